```python
import jax, jax.numpy as jnp
from jax import lax
import numpy as np

D_MODEL = 1024
BATCH = 4
SEQ = 4096
DEPTH = 4

N_META = 16
N_A_LAYERS = DEPTH // 2
N_B_LAYERS = DEPTH - N_A_LAYERS
N_HEADS = 8
HEAD_DIM = D_MODEL // N_HEADS
N_KV_HEADS = 2
GROUP = N_HEADS // N_KV_HEADS
ROPE_THETA = 500000.0
IDX_HEADS = 4
IDX_DIM = 64
INDEX_TOPK = 256
Q_BLOCK = 128
PEER_HEADS = 8
PEER_QUERY_DIM = 256
PEER_HALF = PEER_QUERY_DIM // 2
N_KEYS = 128
N_EXPERTS = N_KEYS * N_KEYS
PEER_TOPK = 16
PEER_CHUNK = 128
RMS_EPS = 1e-6

A_Q = N_HEADS * HEAD_DIM
A_K = N_KV_HEADS * HEAD_DIM
A_V = N_KV_HEADS * HEAD_DIM
A_QI = IDX_HEADS * IDX_DIM
A_KI = IDX_DIM
A_WI = IDX_HEADS
A_COLS = A_Q + A_K + A_V + A_QI + A_KI + A_WI
A_SPLITS = [A_Q, A_Q + A_K, A_Q + A_K + A_V, A_Q + A_K + A_V + A_QI, A_Q + A_K + A_V + A_QI + A_KI]

kernel_name = "yoco_dsa_stickbreak_peer_meta"


def rms_norm(x, gain):
    xf = x.astype(jnp.float32)
    y = xf * lax.rsqrt(jnp.mean(xf * xf, axis=-1, keepdims=True) + RMS_EPS)
    return (y * gain.astype(jnp.float32)).astype(x.dtype)


def partial_rope(x, pos):
    d = x.shape[-1]
    r = d // 4
    half = r // 2
    inv = ROPE_THETA ** (-jnp.arange(half, dtype=jnp.float32) * 2.0 / r)
    ang = pos[:, None] * inv[None, :]
    cos = jnp.cos(ang)[None, :, None, :]
    sin = jnp.sin(ang)[None, :, None, :]
    xf = x.astype(jnp.float32)
    x1 = xf[..., :half]
    x2 = xf[..., half:r]
    out = jnp.concatenate([x1 * cos - x2 * sin, x2 * cos + x1 * sin, xf[..., r:]], axis=-1)
    return out.astype(x.dtype)


def to_blocks(a):
    b, t = a.shape[:2]
    a = a.reshape((b, t // Q_BLOCK, Q_BLOCK) + a.shape[2:])
    return jnp.moveaxis(a, 1, 0)


def from_blocks(a):
    a = jnp.moveaxis(a, 0, 1)
    return a.reshape((a.shape[0], a.shape[1] * a.shape[2]) + a.shape[3:])


def dsa_attention(hn, w_in, q_gain, k_gain, w_o, pos, topk):
    b, t, _ = hn.shape
    proj = hn @ w_in
    q, k, v, qi, ki, wi = jnp.split(proj, A_SPLITS, axis=-1)
    q = partial_rope(rms_norm(q.reshape(b, t, N_HEADS, HEAD_DIM), q_gain), pos)
    k = partial_rope(rms_norm(k.reshape(b, t, N_KV_HEADS, HEAD_DIM), k_gain), pos)
    v = v.reshape(b, t, N_KV_HEADS, HEAD_DIM)
    qi = partial_rope(qi.reshape(b, t, IDX_HEADS, IDX_DIM), pos) * (IDX_DIM ** -0.5)
    ki = partial_rope(ki.reshape(b, t, 1, IDX_DIM), pos)[:, :, 0]
    wi = wi * (IDX_HEADS ** -0.5)
    key_pos = jnp.arange(t)
    gather = jax.vmap(lambda a, i: a[i])

    def block(args):
        qb, qib, wib, t0 = args
        qpos = t0 + jnp.arange(Q_BLOCK)
        causal = key_pos[None, :] <= qpos[:, None]
        score = jnp.einsum('bqhd,bsd->bqhs', qib.astype(jnp.float32), ki.astype(jnp.float32))
        score = jnp.einsum('bqhs,bqh->bqs', jax.nn.relu(score), wib.astype(jnp.float32))
        score = jnp.where(causal[None], score, -jnp.inf)
        _, sel = lax.top_k(score, topk)
        valid = sel <= qpos[None, :, None]
        k_sel = gather(k, sel)
        v_sel = gather(v, sel)
        qg = qb.reshape(b, Q_BLOCK, N_KV_HEADS, GROUP, HEAD_DIM)
        logits = jnp.einsum('bqhgd,bqnhd->bqhgn', qg, k_sel).astype(jnp.float32) * (HEAD_DIM ** -0.5)
        logits = jnp.where(valid[:, :, None, None, :], logits, -jnp.inf)
        p = jax.nn.softmax(logits, axis=-1).astype(v.dtype)
        o = jnp.einsum('bqhgn,bqnhd->bqhgd', p, v_sel)
        return o.reshape(b, Q_BLOCK, N_HEADS * HEAD_DIM)

    starts = jnp.arange(t // Q_BLOCK) * Q_BLOCK
    o = lax.map(block, (to_blocks(q), to_blocks(qi), to_blocks(wi), starts))
    return from_blocks(o) @ w_o


def stick_breaking_attention(hn, w_q, w_o, k, v):
    b, t, _ = hn.shape
    q = (hn @ w_q).reshape(b, t, N_KV_HEADS, GROUP, HEAD_DIM)
    key_pos = jnp.arange(t)

    def block(args):
        qb, t0 = args
        qpos = t0 + jnp.arange(Q_BLOCK)
        causal = key_pos[None, :] < qpos[:, None]
        z = jnp.einsum('bqhgd,bshd->bhgqs', qb, k).astype(jnp.float32) * (HEAD_DIM ** -0.5)
        log_keep = jnp.where(causal, jax.nn.log_sigmoid(-z), 0.0)
        between = lax.cumsum(log_keep, axis=4, reverse=True) - log_keep
        a = jnp.where(causal, jnp.exp(jax.nn.log_sigmoid(z) + between), 0.0).astype(v.dtype)
        o = jnp.einsum('bhgqs,bshd->bqhgd', a, v)
        return o.reshape(b, Q_BLOCK, N_HEADS * HEAD_DIM)

    starts = jnp.arange(t // Q_BLOCK) * Q_BLOCK
    o = lax.map(block, (to_blocks(q), starts))
    return from_blocks(o) @ w_o


def peer_ffn(hn, w_q, subkeys, u, v):
    b, t, d = hn.shape
    n = b * t
    xt = hn.reshape(n, d)
    q = (xt @ w_q).reshape(n, PEER_HEADS, 2, PEER_HALF)
    s = jnp.einsum('nhcd,ckd->nhck', q, subkeys).astype(jnp.float32)
    s_top, i_top = lax.top_k(s, PEER_TOPK)
    cand = s_top[:, :, 0, :, None] + s_top[:, :, 1, None, :]
    cand_id = i_top[:, :, 0, :, None] * N_KEYS + i_top[:, :, 1, None, :]
    best, pick = lax.top_k(cand.reshape(n, PEER_HEADS, PEER_TOPK * PEER_TOPK), PEER_TOPK)
    ids = jnp.take_along_axis(cand_id.reshape(n, PEER_HEADS, PEER_TOPK * PEER_TOPK), pick, axis=-1)
    g = jax.nn.softmax(best, axis=-1).astype(hn.dtype)
    n_chunks = n // PEER_CHUNK

    def chunk(args):
        xc, ic, gc = args
        u_sel = u[ic]
        v_sel = v[ic]
        act = jax.nn.gelu(jnp.einsum('cd,ced->ce', xc, u_sel), approximate=False) * gc
        return jnp.einsum('ce,ced->cd', act, v_sel)

    out = lax.map(chunk, (xt.reshape(n_chunks, PEER_CHUNK, d),
                          ids.reshape(n_chunks, PEER_CHUNK, PEER_HEADS * PEER_TOPK),
                          g.reshape(n_chunks, PEER_CHUNK, PEER_HEADS * PEER_TOPK)))
    return out.reshape(b, t, d)


def setup_inputs(seed: int = 0) -> dict:
    key = jax.random.key(seed)
    ks = jax.random.split(key, 20)
    f32 = jnp.float32
    hd = N_HEADS * HEAD_DIM

    def nrm(k, shape, scale):
        return jax.random.normal(k, shape, f32) * scale

    def gain(k, shape):
        return 1.0 + 0.02 * jax.random.normal(k, shape, f32)

    return {
        "x": nrm(ks[0], (BATCH, SEQ, D_MODEL), 1.0),
        "meta_tokens": nrm(ks[1], (N_META, D_MODEL), 1.0),
        "a_norm": gain(ks[2], (N_A_LAYERS, D_MODEL)),
        "a_w_in": nrm(ks[3], (N_A_LAYERS, D_MODEL, A_COLS), D_MODEL ** -0.5),
        "a_q_norm": gain(ks[4], (N_A_LAYERS, HEAD_DIM)),
        "a_k_norm": gain(ks[5], (N_A_LAYERS, HEAD_DIM)),
        "a_w_o": nrm(ks[6], (N_A_LAYERS, hd, D_MODEL), hd ** -0.5),
        "kv_norm": gain(ks[7], (D_MODEL,)),
        "kv_w": nrm(ks[8], (D_MODEL, 2 * N_KV_HEADS * HEAD_DIM), D_MODEL ** -0.5),
        "b_norm": gain(ks[9], (N_B_LAYERS, D_MODEL)),
        "b_w_q": nrm(ks[10], (N_B_LAYERS, D_MODEL, hd), D_MODEL ** -0.5),
        "b_w_o": nrm(ks[11], (N_B_LAYERS, hd, D_MODEL), hd ** -0.5),
        "ffn_norm": gain(ks[12], (DEPTH, D_MODEL)),
        "peer_w_q": nrm(ks[13], (DEPTH, D_MODEL, PEER_HEADS * PEER_QUERY_DIM), D_MODEL ** -0.5),
        "peer_subkeys": nrm(ks[14], (DEPTH, 2, N_KEYS, PEER_HALF), PEER_HALF ** -0.5),
        "peer_u": nrm(ks[15], (DEPTH, N_EXPERTS, D_MODEL), D_MODEL ** -0.5),
        "peer_v": nrm(ks[16], (DEPTH, N_EXPERTS, D_MODEL), 0.1),
    }


def reference(x, meta_tokens, a_norm, a_w_in, a_q_norm, a_k_norm, a_w_o, kv_norm, kv_w,
              b_norm, b_w_q, b_w_o, ffn_norm, peer_w_q, peer_subkeys, peer_u, peer_v):
    b, s, d = x.shape
    t = s + N_META
    t_pad = -(-t // Q_BLOCK) * Q_BLOCK
    h = jnp.concatenate([jnp.broadcast_to(meta_tokens[None].astype(x.dtype), (b, N_META, d)), x], axis=1)
    h = jnp.pad(h, ((0, 0), (0, t_pad - t), (0, 0)))
    pos = jnp.arange(t_pad, dtype=jnp.float32)
    topk = min(INDEX_TOPK, s // 4)
    shared_k = None
    shared_v = None
    for layer in range(DEPTH):
        if layer < N_A_LAYERS:
            h = h + dsa_attention(rms_norm(h, a_norm[layer]), a_w_in[layer], a_q_norm[layer],
                                  a_k_norm[layer], a_w_o[layer], pos, topk)
        else:
            j = layer - N_A_LAYERS
            h = h + stick_breaking_attention(rms_norm(h, b_norm[j]), b_w_q[j], b_w_o[j], shared_k, shared_v)
        h = h + peer_ffn(rms_norm(h, ffn_norm[layer]), peer_w_q[layer], peer_subkeys[layer],
                         peer_u[layer], peer_v[layer])
        if layer == N_A_LAYERS - 1:
            kv = (rms_norm(h, kv_norm) @ kv_w).reshape(b, t_pad, 2 * N_KV_HEADS, HEAD_DIM)
            shared_k, shared_v = jnp.split(kv, 2, axis=2)
    return h[:, N_META:N_META + s]
```

```python
import functools

import jax
import jax.numpy as jnp
from jax import lax
from jax.experimental import pallas as pl
from jax.experimental.pallas import tpu as pltpu

F32 = jnp.float32
BF16 = jnp.bfloat16
I32 = jnp.int32

D_MODEL = 1024
N_META = 16
N_HEADS = 8
HEAD_DIM = 128
N_KV_HEADS = 2
GROUP = N_HEADS // N_KV_HEADS
ROPE_THETA = 500000.0
IDX_HEADS = 4
IDX_DIM = 64
INDEX_TOPK = 256
Q_BLOCK = 128
PEER_HEADS = 8
PEER_HALF = 128
N_KEYS = 128
PEER_TOPK = 16
RMS_EPS = 1e-6

LANES = 128
INT_MIN = -(2 ** 31)
NEG_BIG = -1e30
VMEM_LIMIT = 48 * 1024 * 1024

A_Q = N_HEADS * HEAD_DIM
A_KV = N_KV_HEADS * HEAD_DIM
A_X = (IDX_HEADS + 1) * LANES
SCALE = HEAD_DIM ** -0.5


def _params(sem):
    return pltpu.CompilerParams(dimension_semantics=sem, vmem_limit_bytes=VMEM_LIMIT)


def _rms(x, gain):
    return x * lax.rsqrt(jnp.mean(x * x, axis=-1, keepdims=True) + RMS_EPS) * gain


def _rms_proj_body(h_ref, g_ref, w_ref, o_ref):
    y = _rms(h_ref[...], g_ref[...]).astype(BF16)
    o_ref[...] = jnp.dot(y, w_ref[...], preferred_element_type=F32).astype(o_ref.dtype)


def rms_proj(h, gain, w, tm, out_dtype):
    n, d = h.shape
    nout = w.shape[1]
    return pl.pallas_call(
        _rms_proj_body,
        grid=(n // tm,),
        in_specs=[pl.BlockSpec((tm, d), lambda i: (i, 0)),
                  pl.BlockSpec((1, d), lambda i: (0, 0)),
                  pl.BlockSpec((d, nout), lambda i: (0, 0))],
        out_specs=pl.BlockSpec((tm, nout), lambda i: (i, 0)),
        out_shape=jax.ShapeDtypeStruct((n, nout), out_dtype),
        compiler_params=_params(("parallel",)),
        name="rms_proj",
    )(h, gain.reshape(1, d), w)


def _peer_q_body(h_ref, g_ref, w_ref, o_ref):
    y = _rms(h_ref[...], g_ref[...]).astype(BF16)
    for hc in range(2 * PEER_HEADS):
        o_ref[hc] = jnp.dot(y, w_ref[:, hc * LANES:(hc + 1) * LANES],
                            preferred_element_type=F32).astype(o_ref.dtype)


def peer_query(h, gain, w, tm):
    n, d = h.shape
    nhc = 2 * PEER_HEADS
    return pl.pallas_call(
        _peer_q_body,
        grid=(n // tm,),
        in_specs=[pl.BlockSpec((tm, d), lambda i: (i, 0)),
                  pl.BlockSpec((1, d), lambda i: (0, 0)),
                  pl.BlockSpec((d, nhc * LANES), lambda i: (0, 0))],
        out_specs=pl.BlockSpec((nhc, tm, LANES), lambda i: (0, i, 0)),
        out_shape=jax.ShapeDtypeStruct((nhc, n, LANES), BF16),
        compiler_params=_params(("parallel",)),
        name="peer_query",
    )(h, gain.reshape(1, d), w)


def _out_proj_body(h_ref, o_ref, w_ref, out_ref):
    out_ref[...] = h_ref[...] + jnp.dot(o_ref[...], w_ref[...], preferred_element_type=F32)


def out_proj(h, o, w, tm):
    n, d = h.shape
    k = o.shape[1]
    return pl.pallas_call(
        _out_proj_body,
        grid=(n // tm,),
        in_specs=[pl.BlockSpec((tm, d), lambda i: (i, 0)),
                  pl.BlockSpec((tm, k), lambda i: (i, 0)),
                  pl.BlockSpec((k, d), lambda i: (0, 0))],
        out_specs=pl.BlockSpec((tm, d), lambda i: (i, 0)),
        out_shape=jax.ShapeDtypeStruct((n, d), F32),
        compiler_params=_params(("parallel",)),
        name="out_proj",
    )(h, o, w)


def _rope(x, c, sa, sb, half):
    return x * c + pltpu.roll(x, LANES - half, 1) * sa + pltpu.roll(x, half, 1) * sb


def _a_proj_body(h_ref, g_ref, w_ref, qg_ref, kg_ref, hc_ref, hsa_ref, hsb_ref,
                 ic_ref, isa_ref, isb_ref, xs_ref, q_ref, k_ref, v_ref, x_ref):
    y = _rms(h_ref[...], g_ref[...]).astype(BF16)
    hc, hsa, hsb = hc_ref[...], hsa_ref[...], hsb_ref[...]
    ic, isa, isb = ic_ref[...], isa_ref[...], isb_ref[...]
    for hh in range(N_HEADS + N_KV_HEADS):
        lo = hh * LANES
        p = jnp.dot(y, w_ref[:, lo:lo + LANES], preferred_element_type=F32)
        gain = qg_ref[...] if hh < N_HEADS else kg_ref[...]
        p = _rope(_rms(p, gain), hc, hsa, hsb, HEAD_DIM // 8)
        if hh < N_HEADS:
            q_ref[:, lo:lo + LANES] = p.astype(BF16)
        else:
            k_ref[:, lo - A_Q:lo - A_Q + LANES] = p.astype(BF16)
    lo = A_Q + A_KV
    v_ref[...] = jnp.dot(y, w_ref[:, lo:lo + A_KV], preferred_element_type=F32).astype(BF16)
    lo += A_KV
    for s in range(IDX_HEADS + 1):
        p = jnp.dot(y, w_ref[:, lo + s * LANES:lo + (s + 1) * LANES], preferred_element_type=F32)
        p = _rope(p, ic, isa, isb, IDX_DIM // 8)
        x_ref[:, s * LANES:(s + 1) * LANES] = p * xs_ref[:, s * LANES:(s + 1) * LANES]


def a_proj(h, gain, w_all, q_gain, k_gain, tabs, xscale, batch, tm):
    n, d = h.shape
    t_pad = n // batch
    nt = t_pad // tm
    row = lambda b, j: (b * nt + j, 0)
    tab = pl.BlockSpec((tm, LANES), lambda b, j: (j, 0))
    const = lambda shape: pl.BlockSpec(shape, lambda b, j: (0, 0))
    return pl.pallas_call(
        _a_proj_body,
        grid=(batch, nt),
        in_specs=[pl.BlockSpec((tm, d), row), const((1, d)), const(w_all.shape),
                  const((1, LANES)), const((1, LANES)), tab, tab, tab, tab, tab, tab,
                  const((1, A_X))],
        out_specs=[pl.BlockSpec((tm, A_Q), row), pl.BlockSpec((tm, A_KV), row),
                   pl.BlockSpec((tm, A_KV), row), pl.BlockSpec((tm, A_X), row)],
        out_shape=[jax.ShapeDtypeStruct((n, A_Q), BF16), jax.ShapeDtypeStruct((n, A_KV), BF16),
                   jax.ShapeDtypeStruct((n, A_KV), BF16), jax.ShapeDtypeStruct((n, A_X), F32)],
        compiler_params=_params(("parallel", "parallel")),
        name="a_proj",
    )(h, gain.reshape(1, d), w_all, q_gain.reshape(1, LANES), k_gain.reshape(1, LANES),
      *tabs, xscale)


def _dsa_body(q_ref, xq_ref, k_ref, v_ref, xk_ref, o_ref,
              key_scr, wib_scr, m_scr, l_scr, acc_scr, *, topk):
    i = pl.program_id(1)
    nch = i + 1
    row = lax.broadcasted_iota(I32, (Q_BLOCK, LANES), 0)
    col = lax.broadcasted_iota(I32, (Q_BLOCK, LANES), 1)
    kslab = IDX_HEADS * LANES

    for hh in range(IDX_HEADS):
        wcol = xq_ref[:, kslab + IDX_DIM + hh:kslab + IDX_DIM + hh + 1]
        wib_scr[hh] = jnp.broadcast_to(wcol, (Q_BLOCK, LANES))

    def score_chunk(c, carry):
        base = pl.multiple_of(c * LANES, LANES)
        kc = xk_ref[pl.ds(base, LANES), kslab:kslab + LANES]
        s = jnp.zeros((Q_BLOCK, LANES), F32)
        for hh in range(IDX_HEADS):
            d = lax.dot_general(xq_ref[:, hh * LANES:(hh + 1) * LANES], kc,
                                (((1,), (1,)), ((), ())),
                                precision=lax.Precision.HIGHEST, preferred_element_type=F32)
            s = s + jnp.maximum(d, 0.0) * wib_scr[hh]
        s = jnp.where(s == 0.0, 0.0, s)
        bits = pltpu.bitcast(s, I32)
        key = jnp.where(bits < 0, bits ^ jnp.int32(0x7FFFFFFF), bits)
        causal = (c * LANES + col) <= (i * Q_BLOCK + row)
        key_scr[c] = jnp.where(causal, key, jnp.int32(INT_MIN))
        return carry

    lax.fori_loop(0, nch, score_chunk, 0)

    def count(pred_fn):
        def body(c, cnt):
            return cnt + jnp.where(pred_fn(key_scr[c]), 1.0, 0.0)
        cnt = lax.fori_loop(0, nch, body, jnp.zeros((Q_BLOCK, LANES), F32))
        return jnp.sum(cnt, axis=1, keepdims=True)

    def bs_pass(b, ans):
        cand = ans + lax.shift_left(jnp.int32(1), jnp.int32(31) - b)
        candb = jnp.broadcast_to(cand, (Q_BLOCK, LANES))
        tot = count(lambda key: key >= candb)
        return jnp.where(tot >= float(topk), cand, ans)

    ans = lax.fori_loop(0, 32, bs_pass, jnp.full((Q_BLOCK, 1), INT_MIN, I32))
    ansb = jnp.broadcast_to(ans, (Q_BLOCK, LANES))
    n_gt = count(lambda key: key > ansb)
    need = jnp.where(ans == jnp.int32(INT_MIN), 0.0, float(topk) - n_gt)
    needb = jnp.broadcast_to(need, (Q_BLOCK, LANES))

    r2 = lax.broadcasted_iota(I32, (LANES, 2 * LANES), 0)
    c2 = lax.broadcasted_iota(I32, (LANES, 2 * LANES), 1)
    pre_tot = jnp.where((r2 < c2) | (c2 >= LANES), 1.0, 0.0).astype(BF16)

    for hh in range(N_HEADS):
        m_scr[hh] = jnp.full((Q_BLOCK, LANES), NEG_BIG, F32)
        l_scr[hh] = jnp.zeros((Q_BLOCK, LANES), F32)
        acc_scr[hh] = jnp.zeros((Q_BLOCK, LANES), F32)

    def att_chunk(c, ties_seen):
        base = pl.multiple_of(c * LANES, LANES)
        key = key_scr[c]
        eq = key == ansb
        pt = jnp.dot(jnp.where(eq, 1.0, 0.0).astype(BF16), pre_tot, preferred_element_type=F32)
        tie_ok = (ties_seen + pt[:, :LANES]) < needb
        sel = jnp.where(key > ansb, 1.0, jnp.where(eq, jnp.where(tie_ok, 1.0, 0.0), 0.0)) > 0.5
        for hh in range(N_HEADS):
            g = hh // GROUP
            s = lax.dot_general(q_ref[:, hh * LANES:(hh + 1) * LANES],
                                k_ref[pl.ds(base, LANES), g * LANES:(g + 1) * LANES],
                                (((1,), (1,)), ((), ())), preferred_element_type=F32) * SCALE
            s = jnp.where(sel, s, NEG_BIG)
            m_prev = m_scr[hh]
            m_new = jnp.maximum(m_prev, jnp.max(s, axis=1, keepdims=True))
            p = jnp.where(sel, jnp.exp(s - m_new), 0.0)
            alpha = jnp.exp(m_prev - m_new)
            l_scr[hh] = alpha * l_scr[hh] + jnp.sum(p, axis=1, keepdims=True)
            acc_scr[hh] = alpha * acc_scr[hh] + jnp.dot(
                p.astype(BF16), v_ref[pl.ds(base, LANES), g * LANES:(g + 1) * LANES],
                preferred_element_type=F32)
            m_scr[hh] = m_new
        return ties_seen + pt[:, LANES:]

    lax.fori_loop(0, nch, att_chunk, jnp.zeros((Q_BLOCK, LANES), F32))

    for hh in range(N_HEADS):
        o_ref[:, hh * LANES:(hh + 1) * LANES] = (acc_scr[hh] / l_scr[hh]).astype(BF16)


def dsa_attention(q, k, v, xi, batch, topk):
    n = q.shape[0]
    t_pad = n // batch
    nq = t_pad // Q_BLOCK
    blk = lambda b, i: (b * nq + i, 0)
    full = lambda b, i: (b, 0)
    return pl.pallas_call(
        functools.partial(_dsa_body, topk=topk),
        grid=(batch, nq),
        in_specs=[pl.BlockSpec((Q_BLOCK, A_Q), blk), pl.BlockSpec((Q_BLOCK, A_X), blk),
                  pl.BlockSpec((t_pad, A_KV), full), pl.BlockSpec((t_pad, A_KV), full),
                  pl.BlockSpec((t_pad, A_X), full)],
        out_specs=pl.BlockSpec((Q_BLOCK, A_Q), blk),
        out_shape=jax.ShapeDtypeStruct((n, A_Q), BF16),
        scratch_shapes=[pltpu.VMEM((nq, Q_BLOCK, LANES), I32),
                        pltpu.VMEM((IDX_HEADS, Q_BLOCK, LANES), F32),
                        pltpu.VMEM((N_HEADS, Q_BLOCK, LANES), F32),
                        pltpu.VMEM((N_HEADS, Q_BLOCK, LANES), F32),
                        pltpu.VMEM((N_HEADS, Q_BLOCK, LANES), F32)],
        compiler_params=_params(("parallel", "arbitrary")),
        name="dsa_attention",
    )(q, xi, k, v, xi)


def _sb_body(q_ref, k_ref, v_ref, o_ref, run_scr, acc_scr):
    i = pl.program_id(1)
    row = lax.broadcasted_iota(I32, (Q_BLOCK, LANES), 0)
    col = lax.broadcasted_iota(I32, (Q_BLOCK, LANES), 1)
    r2 = lax.broadcasted_iota(I32, (2 * LANES, 2 * LANES), 0) % LANES
    c2 = lax.broadcasted_iota(I32, (2 * LANES, 2 * LANES), 1)
    suf_tot = jnp.where((r2 > c2) | (c2 >= LANES), 1.0, 0.0).astype(BF16)

    for hh in range(N_HEADS):
        run_scr[hh] = jnp.zeros((Q_BLOCK, LANES), F32)
        acc_scr[hh] = jnp.zeros((Q_BLOCK, LANES), F32)

    def chunk(t, carry):
        c = i - t
        base = pl.multiple_of(c * LANES, LANES)
        causal = (c * LANES + col) < (i * Q_BLOCK + row)
        for hh in range(N_HEADS):
            g = hh // GROUP
            z = lax.dot_general(q_ref[:, hh * LANES:(hh + 1) * LANES],
                                k_ref[pl.ds(base, LANES), g * LANES:(g + 1) * LANES],
                                (((1,), (1,)), ((), ())), preferred_element_type=F32) * SCALE
            sp = jnp.maximum(z, 0.0) + jnp.log1p(jnp.exp(-jnp.abs(z)))
            lk = jnp.where(causal, -sp, 0.0)
            hi = lk.astype(BF16)
            lo = (lk - hi.astype(F32)).astype(BF16)
            st = jnp.dot(jnp.concatenate([hi, lo], axis=1), suf_tot, preferred_element_type=F32)
            between = st[:, :LANES] + run_scr[hh]
            a = jnp.where(causal, jnp.exp(z - sp + between), 0.0)
            acc_scr[hh] = acc_scr[hh] + jnp.dot(
                a.astype(BF16), v_ref[pl.ds(base, LANES), g * LANES:(g + 1) * LANES],
                preferred_element_type=F32)
            run_scr[hh] = run_scr[hh] + st[:, LANES:]
        return carry

    lax.fori_loop(0, i + 1, chunk, 0)

    for hh in range(N_HEADS):
        o_ref[:, hh * LANES:(hh + 1) * LANES] = acc_scr[hh].astype(BF16)


def sb_attention(q, kv, batch):
    n = q.shape[0]
    t_pad = n // batch
    nq = t_pad // Q_BLOCK
    blk = lambda b, i: (b * nq + i, 0)
    return pl.pallas_call(
        _sb_body,
        grid=(batch, nq),
        in_specs=[pl.BlockSpec((Q_BLOCK, A_Q), blk),
                  pl.BlockSpec((t_pad, A_KV), lambda b, i: (b, 0)),
                  pl.BlockSpec((t_pad, A_KV), lambda b, i: (b, 1))],
        out_specs=pl.BlockSpec((Q_BLOCK, A_Q), blk),
        out_shape=jax.ShapeDtypeStruct((n, A_Q), BF16),
        scratch_shapes=[pltpu.VMEM((N_HEADS, Q_BLOCK, LANES), F32),
                        pltpu.VMEM((N_HEADS, Q_BLOCK, LANES), F32)],
        compiler_params=_params(("parallel", "arbitrary")),
        name="sb_attention",
    )(q, kv, kv)


def _extract_top(s, kidx, pidx):
    cur = s
    rank = jnp.full(s.shape, float(PEER_TOPK), F32)
    tops = jnp.zeros((PEER_TOPK, s.shape[1]), F32)
    for r in range(PEER_TOPK):
        m = jnp.max(cur, axis=0, keepdims=True)
        first = jnp.min(jnp.where(cur == m, kidx, float(N_KEYS)), axis=0, keepdims=True)
        hit = kidx == first
        rank = jnp.where(hit, float(r), rank)
        cur = jnp.where(hit, -jnp.inf, cur)
        tops = jnp.where(pidx == float(r), m, tops)
    return rank, tops


def _route_body(q_ref, sk_ref, e1_ref, lim_ref, e2_ref, r2_ref):
    kidx = lax.broadcasted_iota(I32, (N_KEYS, LANES), 0).astype(F32)
    pidx = lax.broadcasted_iota(I32, (PEER_TOPK, LANES), 0).astype(F32)

    def head(hh, carry):
        nt = (((1,), (1,)), ((), ()))
        s1 = lax.dot_general(sk_ref[0], q_ref[2 * hh], nt, preferred_element_type=F32)
        s2 = lax.dot_general(sk_ref[1], q_ref[2 * hh + 1], nt, preferred_element_type=F32)
        rank1, a = _extract_top(s1, kidx, pidx)
        rank2, b = _extract_top(s2, kidx, pidx)

        def take(_, st):
            qmax, head_val = st
            m = jnp.max(head_val, axis=0, keepdims=True)
            first = jnp.min(jnp.where(head_val == m, pidx, float(PEER_TOPK)), axis=0, keepdims=True)
            hit = pidx == first
            qmax = jnp.where(hit, qmax + 1.0, qmax)
            qsel = jnp.sum(jnp.where(hit, qmax, 0.0), axis=0, keepdims=True)
            bnext = jnp.sum(jnp.where(pidx == qsel, b, 0.0), axis=0, keepdims=True)
            bnext = jnp.where(qsel >= float(PEER_TOPK), -jnp.inf, bnext)
            return qmax, jnp.where(hit, a + bnext, head_val)

        qmax, _ = lax.fori_loop(0, PEER_TOPK, take,
                                (jnp.zeros((PEER_TOPK, LANES), F32), a + b[0:1, :]),
                                unroll=True)

        ea = jnp.exp(a - a[0:1, :])
        eb = jnp.exp(b - b[0:1, :])
        pref = jnp.zeros((PEER_TOPK, LANES), F32)
        run = jnp.zeros((1, LANES), F32)
        for q in range(PEER_TOPK):
            run = run + eb[q:q + 1, :]
            pref = jnp.where(qmax == float(q + 1), run, pref)
        z = jnp.sum(ea * pref, axis=0, keepdims=True)

        lim = jnp.zeros((N_KEYS, LANES), F32)
        for p in range(PEER_TOPK):
            lim = jnp.where(rank1 == float(p), qmax[p:p + 1, :], lim)
        e1_ref[hh] = jnp.where(rank1 < float(PEER_TOPK), jnp.exp(s1 - a[0:1, :]), 0.0) / z
        lim_ref[hh] = lim
        e2_ref[hh] = jnp.where(rank2 < float(PEER_TOPK), jnp.exp(s2 - b[0:1, :]), 0.0)
        r2_ref[hh] = rank2
        return carry

    lax.fori_loop(0, PEER_HEADS, head, 0)


def peer_route(q, subkeys):
    nhc, n, _ = q.shape
    tab = pl.BlockSpec((PEER_HEADS, N_KEYS, LANES), lambda t: (0, 0, t))
    shape = jax.ShapeDtypeStruct((PEER_HEADS, N_KEYS, n), F32)
    return pl.pallas_call(
        _route_body,
        grid=(n // LANES,),
        in_specs=[pl.BlockSpec((nhc, LANES, LANES), lambda t: (0, t, 0)),
                  pl.BlockSpec((2, N_KEYS, PEER_HALF), lambda t: (0, 0, 0))],
        out_specs=[tab, tab, tab, tab],
        out_shape=[shape, shape, shape, shape],
        compiler_params=_params(("parallel",)),
        name="peer_route",
    )(q, subkeys)


def _gelu(x):
    return 0.5 * x * (1.0 + lax.erf(x * (2.0 ** -0.5)))


def _expert_body(h_ref, g_ref, u_ref, v_ref, e1_ref, lim_ref, e2_ref, r2_ref, out_ref,
                 xt_scr, acc_scr, ht_scr, pt_scr, *, tm, eb):
    e = pl.program_id(1)

    @pl.when(e == 0)
    def _():
        y = _rms(h_ref[...], g_ref[...])
        xt_scr[...] = y.T.astype(BF16)
        acc_scr[...] = jnp.zeros_like(acc_scr)

    ht_scr[...] = jnp.dot(u_ref[...], xt_scr[...], preferred_element_type=F32)
    for ii in range(eb // N_KEYS):
        i = e * (eb // N_KEYS) + ii
        e1_i = e1_ref[i]
        lim_i = lim_ref[i]
        for tc in range(tm // LANES):
            sl = slice(tc * LANES, (tc + 1) * LANES)
            gate = jnp.zeros((N_KEYS, LANES), F32)
            for hh in range(PEER_HEADS):
                e1row = e1_i[hh:hh + 1, sl]
                limrow = lim_i[hh:hh + 1, sl]
                gate = gate + jnp.where(r2_ref[hh, :, sl] < limrow, e2_ref[hh, :, sl], 0.0) * e1row
            p = _gelu(ht_scr[ii * N_KEYS:(ii + 1) * N_KEYS, sl]) * gate
            pt_scr[ii * N_KEYS:(ii + 1) * N_KEYS, sl] = p.astype(BF16)
    acc_scr[...] += lax.dot_general(pt_scr[...], v_ref[...], (((0,), (0,)), ((), ())),
                                    preferred_element_type=F32)

    @pl.when(e == pl.num_programs(1) - 1)
    def _():
        out_ref[...] = h_ref[...] + acc_scr[...]


def peer_experts(h, gain, u, v, tabs, tm, eb):
    n, d = h.shape
    ne = u.shape[0]
    tab = pl.BlockSpec((PEER_HEADS, N_KEYS, tm), lambda t, e: (0, 0, t))
    rtab = pl.BlockSpec((N_KEYS, PEER_HEADS, tm), lambda t, e: (0, 0, t))
    e1, lim, e2, r2 = tabs
    tabs = (jnp.transpose(e1, (1, 0, 2)), jnp.transpose(lim, (1, 0, 2)), e2, r2)
    return pl.pallas_call(
        functools.partial(_expert_body, tm=tm, eb=eb),
        grid=(n // tm, ne // eb),
        in_specs=[pl.BlockSpec((tm, d), lambda t, e: (t, 0)),
                  pl.BlockSpec((1, d), lambda t, e: (0, 0)),
                  pl.BlockSpec((eb, d), lambda t, e: (e, 0)),
                  pl.BlockSpec((eb, d), lambda t, e: (e, 0)),
                  rtab, rtab, tab, tab],
        out_specs=pl.BlockSpec((tm, d), lambda t, e: (t, 0)),
        out_shape=jax.ShapeDtypeStruct((n, d), F32),
        scratch_shapes=[pltpu.VMEM((d, tm), BF16), pltpu.VMEM((tm, d), F32),
                        pltpu.VMEM((eb, tm), F32), pltpu.VMEM((eb, tm), BF16)],
        compiler_params=_params(("parallel", "arbitrary")),
        name="peer_experts",
    )(h, gain.reshape(1, d), u, v, *tabs)


def peer_ffn(h, gain, w_q, subkeys, u, v, tm_q, tm_e, eb):
    q = peer_query(h, gain, w_q, tm_q)
    tabs = peer_route(q, subkeys)
    return peer_experts(h, gain, u, v, tabs, tm_e, eb)


def _rope_tables(t_pad, dim):
    r = dim // 4
    half = r // 2
    pos = jnp.arange(t_pad, dtype=F32)
    inv = ROPE_THETA ** (-jnp.arange(half, dtype=F32) * 2.0 / r)
    ang = pos[:, None] * inv[None, :]
    cos, sin = jnp.cos(ang), jnp.sin(ang)
    pad = LANES - 2 * half
    c = jnp.concatenate([cos, cos, jnp.ones((t_pad, pad), F32)], axis=1)
    sa = jnp.concatenate([-sin, jnp.zeros((t_pad, half + pad), F32)], axis=1)
    sb = jnp.concatenate([jnp.zeros((t_pad, half), F32), sin, jnp.zeros((t_pad, pad), F32)], axis=1)
    return c, sa, sb


def _a_weights(w_in):
    d = w_in.shape[0]
    lo = A_Q + 2 * A_KV
    zeros = jnp.zeros((d, LANES - IDX_DIM), w_in.dtype)
    cols = [w_in[:, :lo]]
    for hh in range(IDX_HEADS):
        cols += [w_in[:, lo + hh * IDX_DIM:lo + (hh + 1) * IDX_DIM], zeros]
    lo += IDX_HEADS * IDX_DIM
    cols += [w_in[:, lo:lo + IDX_DIM + IDX_HEADS],
             jnp.zeros((d, LANES - IDX_DIM - IDX_HEADS), w_in.dtype)]
    return jnp.concatenate(cols, axis=1).astype(BF16)


def _x_scale():
    s = jnp.full((A_X,), IDX_DIM ** -0.5, F32)
    last = jnp.ones((LANES,), F32).at[IDX_DIM:IDX_DIM + IDX_HEADS].set(IDX_HEADS ** -0.5)
    return s.at[IDX_HEADS * LANES:].set(last).reshape(1, A_X)


def kernel(x, meta_tokens, a_norm, a_w_in, a_q_norm, a_k_norm, a_w_o, kv_norm, kv_w,
           b_norm, b_w_q, b_w_o, ffn_norm, peer_w_q, peer_subkeys, peer_u, peer_v):
    b, s, d = x.shape
    t = s + N_META
    t_pad = -(-t // Q_BLOCK) * Q_BLOCK
    n = b * t_pad
    depth = ffn_norm.shape[0]
    n_a = a_norm.shape[0]
    topk = min(INDEX_TOPK, s // 4)

    h = jnp.concatenate([jnp.broadcast_to(meta_tokens[None].astype(x.dtype), (b, N_META, d)), x], axis=1)
    h = jnp.pad(h, ((0, 0), (0, t_pad - t), (0, 0))).reshape(n, d)

    tabs = _rope_tables(t_pad, HEAD_DIM) + _rope_tables(t_pad, IDX_DIM)
    xscale = _x_scale()
    tm_a = t_pad // 11 if t_pad % 11 == 0 and (t_pad // 11) % 8 == 0 else Q_BLOCK
    tm = 512 if n % 512 == 0 else Q_BLOCK

    kv = None
    for layer in range(depth):
        if layer < n_a:
            q, k, v, xi = a_proj(h, a_norm[layer], _a_weights(a_w_in[layer]), a_q_norm[layer],
                                 a_k_norm[layer], tabs, xscale, b, tm_a)
            o = dsa_attention(q, k, v, xi, b, topk)
            h = out_proj(h, o, a_w_o[layer].astype(BF16), tm)
        else:
            j = layer - n_a
            q = rms_proj(h, b_norm[j], b_w_q[j].astype(BF16), tm, BF16)
            o = sb_attention(q, kv, b)
            h = out_proj(h, o, b_w_o[j].astype(BF16), tm)
        h = peer_ffn(h, ffn_norm[layer], peer_w_q[layer].astype(BF16),
                     peer_subkeys[layer].astype(BF16), peer_u[layer].astype(BF16),
                     peer_v[layer].astype(BF16), tm, tm, 2 * N_KEYS)
        if layer == n_a - 1:
            kv = rms_proj(h, kv_norm, kv_w.astype(BF16), tm, BF16)
    return h.reshape(b, t_pad, d)[:, N_META:N_META + s]
```

```python
import functools

import jax
import jax.numpy as jnp
from jax import lax
from jax.experimental import pallas as pl
from jax.experimental.pallas import tpu as pltpu

F32 = jnp.float32
BF16 = jnp.bfloat16
I32 = jnp.int32
U32 = jnp.uint32

D_MODEL = 1024
N_META = 16
N_HEADS = 8
HEAD_DIM = 128
N_KV_HEADS = 2
GROUP = N_HEADS // N_KV_HEADS
ROPE_THETA = 500000.0
IDX_HEADS = 4
IDX_DIM = 64
INDEX_TOPK = 256
Q_BLOCK = 128
PEER_HEADS = 8
PEER_HALF = 128
N_KEYS = 128
PEER_TOPK = 16
RMS_EPS = 1e-6

LANES = 128
KCH = 2 * LANES
INT_MIN = -(2 ** 31)
NEG_BIG = -1e30
EXP_UNDERFLOW = 104.0
VMEM_LIMIT = 48 * 1024 * 1024

A_Q = N_HEADS * HEAD_DIM
A_KV = N_KV_HEADS * HEAD_DIM
A_AI = IDX_HEADS * KCH
A_KB = KCH
GROWS = GROUP * Q_BLOCK
SCALE = HEAD_DIM ** -0.5
NT_DIMS = (((1,), (1,)), ((), ()))


def _params(sem):
    return pltpu.CompilerParams(dimension_semantics=sem, vmem_limit_bytes=VMEM_LIMIT)


def _rms(x, gain):
    return x * lax.rsqrt(jnp.mean(x * x, axis=-1, keepdims=True) + RMS_EPS) * gain


def _rms_proj_body(h_ref, g_ref, w_ref, o_ref):
    y = _rms(h_ref[...], g_ref[...]).astype(BF16)
    o_ref[...] = jnp.dot(y, w_ref[...], preferred_element_type=F32).astype(o_ref.dtype)


def rms_proj(h, gain, w, tm, out_dtype):
    n, d = h.shape
    nout = w.shape[1]
    return pl.pallas_call(
        _rms_proj_body,
        grid=(n // tm,),
        in_specs=[pl.BlockSpec((tm, d), lambda i: (i, 0)),
                  pl.BlockSpec((1, d), lambda i: (0, 0)),
                  pl.BlockSpec((d, nout), lambda i: (0, 0))],
        out_specs=pl.BlockSpec((tm, nout), lambda i: (i, 0)),
        out_shape=jax.ShapeDtypeStruct((n, nout), out_dtype),
        compiler_params=_params(("parallel",)),
        name="rms_proj",
    )(h, gain.reshape(1, d), w)


def _peer_q_body(h_ref, g_ref, w_ref, o_ref):
    y = _rms(h_ref[...], g_ref[...]).astype(BF16)
    for hc in range(2 * PEER_HEADS):
        o_ref[hc] = jnp.dot(y, w_ref[:, hc * LANES:(hc + 1) * LANES],
                            preferred_element_type=F32).astype(o_ref.dtype)


def peer_query(h, gain, w, tm):
    n, d = h.shape
    nhc = 2 * PEER_HEADS
    return pl.pallas_call(
        _peer_q_body,
        grid=(n // tm,),
        in_specs=[pl.BlockSpec((tm, d), lambda i: (i, 0)),
                  pl.BlockSpec((1, d), lambda i: (0, 0)),
                  pl.BlockSpec((d, nhc * LANES), lambda i: (0, 0))],
        out_specs=pl.BlockSpec((nhc, tm, LANES), lambda i: (0, i, 0)),
        out_shape=jax.ShapeDtypeStruct((nhc, n, LANES), BF16),
        compiler_params=_params(("parallel",)),
        name="peer_query",
    )(h, gain.reshape(1, d), w)


def _out_proj_body(h_ref, o_ref, w_ref, out_ref):
    out_ref[...] = h_ref[...] + jnp.dot(o_ref[...], w_ref[...], preferred_element_type=F32)


def out_proj(h, o, w, tm):
    n, d = h.shape
    k = o.shape[1]
    return pl.pallas_call(
        _out_proj_body,
        grid=(n // tm,),
        in_specs=[pl.BlockSpec((tm, d), lambda i: (i, 0)),
                  pl.BlockSpec((tm, k), lambda i: (i, 0)),
                  pl.BlockSpec((k, d), lambda i: (0, 0))],
        out_specs=pl.BlockSpec((tm, d), lambda i: (i, 0)),
        out_shape=jax.ShapeDtypeStruct((n, d), F32),
        compiler_params=_params(("parallel",)),
        name="out_proj",
    )(h, o, w)


def _rope(x, c, sa, sb, half):
    return x * c + pltpu.roll(x, LANES - half, 1) * sa + pltpu.roll(x, half, 1) * sb


def _a_proj_body(h_ref, g_ref, w_ref, qg_ref, kg_ref, hc_ref, hsa_ref, hsb_ref,
                 ic_ref, isa_ref, isb_ref, q_ref, k_ref, v_ref, ai_ref, kb_ref, wi_ref):
    y = _rms(h_ref[...], g_ref[...]).astype(BF16)
    hc, hsa, hsb = hc_ref[...], hsa_ref[...], hsb_ref[...]
    ic, isa, isb = ic_ref[...], isa_ref[...], isb_ref[...]
    for hh in range(N_HEADS + N_KV_HEADS):
        lo = hh * LANES
        p = jnp.dot(y, w_ref[:, lo:lo + LANES], preferred_element_type=F32)
        gain = qg_ref[...] if hh < N_HEADS else kg_ref[...]
        p = _rope(_rms(p, gain), hc, hsa, hsb, HEAD_DIM // 8)
        if hh < N_HEADS:
            q_ref[:, lo:lo + LANES] = p.astype(BF16)
        else:
            k_ref[:, lo - A_Q:lo - A_Q + LANES] = p.astype(BF16)
    lo = A_Q + A_KV
    v_ref[...] = jnp.dot(y, w_ref[:, lo:lo + A_KV], preferred_element_type=F32).astype(BF16)
    lo += A_KV
    first_group = lax.broadcasted_iota(I32, (y.shape[0], LANES), 1) < IDX_DIM
    n_q_slabs = A_AI // LANES
    for s in range(n_q_slabs + A_KB // LANES):
        p = jnp.dot(y, w_ref[:, lo + s * LANES:lo + (s + 1) * LANES], preferred_element_type=F32)
        p = _rope(p, ic, isa, isb, IDX_DIM // 8)
        if s < n_q_slabs:
            p = p * (IDX_DIM ** -0.5)
        hi = p.astype(BF16).astype(F32)
        if s < n_q_slabs:
            val = jnp.where(first_group, hi, p - hi) if s % 2 == 0 else hi
            ai_ref[:, s * LANES:(s + 1) * LANES] = val.astype(BF16)
        else:
            val = hi if s == n_q_slabs else p - hi
            kb_ref[:, (s - n_q_slabs) * LANES:(s - n_q_slabs + 1) * LANES] = val.astype(BF16)
    lo += A_AI + A_KB
    wi_ref[...] = jnp.dot(y, w_ref[:, lo:lo + LANES], preferred_element_type=F32) * (IDX_HEADS ** -0.5)


def a_proj(h, gain, w_all, q_gain, k_gain, tabs, batch, tm):
    n, d = h.shape
    t_pad = n // batch
    nt = t_pad // tm
    row = lambda b, j: (b * nt + j, 0)
    tab = pl.BlockSpec((tm, LANES), lambda b, j: (j, 0))
    const = lambda shape: pl.BlockSpec(shape, lambda b, j: (0, 0))
    widths = (A_Q, A_KV, A_KV, A_AI, A_KB)
    return pl.pallas_call(
        _a_proj_body,
        grid=(batch, nt),
        in_specs=[pl.BlockSpec((tm, d), row), const((1, d)), const(w_all.shape),
                  const((1, LANES)), const((1, LANES)), tab, tab, tab, tab, tab, tab],
        out_specs=[pl.BlockSpec((tm, w), row) for w in widths] + [pl.BlockSpec((tm, LANES), row)],
        out_shape=[jax.ShapeDtypeStruct((n, w), BF16) for w in widths]
        + [jax.ShapeDtypeStruct((n, LANES), F32)],
        compiler_params=_params(("parallel", "parallel")),
        name="a_proj",
    )(h, gain.reshape(1, d), w_all, q_gain.reshape(1, LANES), k_gain.reshape(1, LANES), *tabs)


def _dsa_body(q_ref, ai_ref, wi_ref, k_ref, ve_ref, kb_ref, o_ref,
              key_scr, bias_scr, wib_scr, qs_scr, mx_scr, acc_scr, *, topk):
    i = pl.program_id(1)
    nch = i + 1
    nch2 = (i + 2) // 2
    row2 = lax.broadcasted_iota(I32, (Q_BLOCK, KCH), 0)
    col2 = lax.broadcasted_iota(I32, (Q_BLOCK, KCH), 1)

    for hh in range(IDX_HEADS):
        wib_scr[hh] = jnp.broadcast_to(wi_ref[:, hh:hh + 1], (Q_BLOCK, KCH))
    for hh in range(N_HEADS):
        g, j = divmod(hh, GROUP)
        qs_scr[g, j * Q_BLOCK:(j + 1) * Q_BLOCK, :] = q_ref[:, hh * LANES:(hh + 1) * LANES]

    def score_chunk(c, carry):
        base = pl.multiple_of(c * KCH, KCH)
        kc = kb_ref[pl.ds(base, KCH), :]
        s = jnp.zeros((Q_BLOCK, KCH), F32)
        for hh in range(IDX_HEADS):
            d = lax.dot_general(ai_ref[:, hh * KCH:(hh + 1) * KCH], kc, NT_DIMS,
                                preferred_element_type=F32)
            s = s + jnp.maximum(d, 0.0) * wib_scr[hh]
        s = jnp.where(s == 0.0, 0.0, s)
        bits = pltpu.bitcast(s, I32)
        key = jnp.where(bits < 0, bits ^ jnp.int32(0x7FFFFFFF), bits)
        causal = (c * KCH + col2) <= (i * Q_BLOCK + row2)
        key = jnp.where(causal, key, jnp.int32(INT_MIN))
        key_scr[2 * c] = key[:, :LANES]
        key_scr[2 * c + 1] = key[:, LANES:]
        return carry

    lax.fori_loop(0, nch2, score_chunk, 0)

    def count(pred_fn):
        def body(c, cnt):
            return cnt + jnp.where(pred_fn(key_scr[c]), 1.0, 0.0)
        cnt = lax.fori_loop(0, nch, body, jnp.zeros((Q_BLOCK, LANES), F32))
        return jnp.sum(cnt, axis=1, keepdims=True)

    def bs_pass(b, ans):
        cand = ans + lax.shift_left(jnp.int32(1), jnp.int32(31) - b)
        candb = jnp.broadcast_to(cand, (Q_BLOCK, LANES))
        tot = count(lambda key: key >= candb)
        return jnp.where(tot >= float(topk), cand, ans)

    ans = lax.fori_loop(0, 32, bs_pass, jnp.full((Q_BLOCK, 1), INT_MIN, I32))
    ansb = jnp.broadcast_to(ans, (Q_BLOCK, LANES))
    n_gt = count(lambda key: key > ansb)
    need = jnp.where(ans == jnp.int32(INT_MIN), 0.0, float(topk) - n_gt)
    needb = jnp.broadcast_to(need, (Q_BLOCK, LANES))

    r2 = lax.broadcasted_iota(I32, (LANES, 2 * LANES), 0)
    c2 = lax.broadcasted_iota(I32, (LANES, 2 * LANES), 1)
    pre_tot = jnp.where((r2 < c2) | (c2 >= LANES), 1.0, 0.0).astype(BF16)

    def bias_chunk(c, ties_seen):
        key = key_scr[c]
        eq = key == ansb
        pt = jnp.dot(jnp.where(eq, 1.0, 0.0).astype(BF16), pre_tot, preferred_element_type=F32)
        tie_ok = (ties_seen + pt[:, :LANES]) < needb
        bias_scr[c] = jnp.where(key > ansb, 0.0,
                                jnp.where(eq, jnp.where(tie_ok, 0.0, NEG_BIG), NEG_BIG))
        return ties_seen + pt[:, LANES:]

    lax.fori_loop(0, nch, bias_chunk, jnp.zeros((Q_BLOCK, LANES), F32))
    bias_scr[nch] = jnp.full((Q_BLOCK, LANES), NEG_BIG, F32)

    def logits(c, g, bias4):
        base = pl.multiple_of(c * KCH, KCH)
        s = lax.dot_general(qs_scr[g], k_ref[pl.ds(base, KCH), g * LANES:(g + 1) * LANES],
                            NT_DIMS, preferred_element_type=F32)
        return s * SCALE + bias4

    def bias_rows(c):
        b2 = jnp.concatenate([bias_scr[2 * c], bias_scr[2 * c + 1]], axis=1)
        return jnp.tile(b2, (GROUP, 1))

    mx_scr[...] = jnp.full(mx_scr.shape, NEG_BIG, F32)

    def max_chunk(c, carry):
        bias4 = bias_rows(c)
        for g in range(N_KV_HEADS):
            s = logits(c, g, bias4)
            mx_scr[g] = jnp.maximum(mx_scr[g], jnp.maximum(s[:, :LANES], s[:, LANES:]))
        return carry

    lax.fori_loop(0, nch2, max_chunk, 0)
    for g in range(N_KV_HEADS):
        mx_scr[g] = jnp.broadcast_to(jnp.max(mx_scr[g], axis=1, keepdims=True), (GROWS, LANES))

    acc_scr[...] = jnp.zeros_like(acc_scr)

    def acc_chunk(c, carry):
        bias4 = bias_rows(c)
        base = pl.multiple_of(c * KCH, KCH)
        for g in range(N_KV_HEADS):
            m = mx_scr[g]
            p = jnp.exp(logits(c, g, bias4) - jnp.concatenate([m, m], axis=1))
            acc_scr[g] += jnp.dot(p.astype(BF16), ve_ref[pl.ds(base, KCH), g * KCH:(g + 1) * KCH],
                                  preferred_element_type=F32)
        return carry

    lax.fori_loop(0, nch2, acc_chunk, 0)

    for hh in range(N_HEADS):
        g, j = divmod(hh, GROUP)
        a = acc_scr[g, j * Q_BLOCK:(j + 1) * Q_BLOCK, :]
        o_ref[:, hh * LANES:(hh + 1) * LANES] = (a[:, :LANES] / a[:, LANES:]).astype(BF16)


def _pad_keys(a, batch, tk):
    t_pad = a.shape[0] // batch
    a = a.reshape(batch, t_pad, a.shape[1])
    return jnp.pad(a, ((0, 0), (0, tk - t_pad), (0, 0))).reshape(batch * tk, a.shape[2])


def dsa_attention(q, k, v, ai, kb, wi, batch, topk):
    n = q.shape[0]
    t_pad = n // batch
    nq = t_pad // Q_BLOCK
    tk = ((nq + 1) // 2) * KCH
    ones = jnp.ones((n, LANES), v.dtype)
    ve = jnp.concatenate([v[:, :LANES], ones, v[:, LANES:], ones], axis=1)
    k, ve, kb = _pad_keys(k, batch, tk), _pad_keys(ve, batch, tk), _pad_keys(kb, batch, tk)
    blk = lambda b, i: (b * nq + i, 0)
    full = lambda b, i: (b, 0)
    return pl.pallas_call(
        functools.partial(_dsa_body, topk=topk),
        grid=(batch, nq),
        in_specs=[pl.BlockSpec((Q_BLOCK, A_Q), blk), pl.BlockSpec((Q_BLOCK, A_AI), blk),
                  pl.BlockSpec((Q_BLOCK, LANES), blk),
                  pl.BlockSpec((tk, A_KV), full), pl.BlockSpec((tk, 2 * KCH), full),
                  pl.BlockSpec((tk, A_KB), full)],
        out_specs=pl.BlockSpec((Q_BLOCK, A_Q), blk),
        out_shape=jax.ShapeDtypeStruct((n, A_Q), BF16),
        scratch_shapes=[pltpu.VMEM((nq + 2, Q_BLOCK, LANES), I32),
                        pltpu.VMEM((nq + 2, Q_BLOCK, LANES), F32),
                        pltpu.VMEM((IDX_HEADS, Q_BLOCK, KCH), F32),
                        pltpu.VMEM((N_KV_HEADS, GROWS, LANES), BF16),
                        pltpu.VMEM((N_KV_HEADS, GROWS, LANES), F32),
                        pltpu.VMEM((N_KV_HEADS, GROWS, KCH), F32)],
        compiler_params=_params(("parallel", "arbitrary")),
        name="dsa_attention",
    )(q, ai, wi, k, ve, kb)


def _sb_body(q_ref, k_ref, v_ref, o_ref, qs_scr, run_scr, acc_scr):
    i = pl.program_id(1)
    for hh in range(N_HEADS):
        g, j = divmod(hh, GROUP)
        qs_scr[g, j * Q_BLOCK:(j + 1) * Q_BLOCK, :] = q_ref[:, hh * LANES:(hh + 1) * LANES]
    run_scr[...] = jnp.zeros_like(run_scr)
    acc_scr[...] = jnp.zeros_like(acc_scr)

    r2 = lax.broadcasted_iota(I32, (2 * LANES, 2 * LANES), 0) % LANES
    c2 = lax.broadcasted_iota(I32, (2 * LANES, 2 * LANES), 1)
    suf_tot = jnp.where((r2 > c2) | (c2 >= LANES), 1.0, 0.0).astype(BF16)

    def step(c, causal):
        base = pl.multiple_of(c * LANES, LANES)
        for g in range(N_KV_HEADS):
            z = lax.dot_general(qs_scr[g], k_ref[pl.ds(base, LANES), g * LANES:(g + 1) * LANES],
                                NT_DIMS, preferred_element_type=F32) * SCALE
            sp = jnp.maximum(z, 0.0) + jnp.log1p(jnp.exp(-jnp.abs(z)))
            lk = -sp if causal is None else jnp.where(causal, -sp, 0.0)
            hi = lk.astype(BF16)
            lo = (lk - hi.astype(F32)).astype(BF16)
            st = jnp.dot(jnp.concatenate([hi, lo], axis=1), suf_tot, preferred_element_type=F32)
            a = jnp.exp(z - sp + st[:, :LANES] + run_scr[g])
            if causal is not None:
                a = jnp.where(causal, a, 0.0)
            acc_scr[g] += jnp.dot(a.astype(BF16), v_ref[pl.ds(base, LANES), g * LANES:(g + 1) * LANES],
                                  preferred_element_type=F32)
            run_scr[g] += st[:, LANES:]

    row = lax.broadcasted_iota(I32, (GROWS, LANES), 0) % Q_BLOCK
    col = lax.broadcasted_iota(I32, (GROWS, LANES), 1)
    step(i, col < row)

    def cond(state):
        t, live = state
        return jnp.logical_and(t <= i, live)

    def body(state):
        t, _ = state
        step(i - t, None)
        worst = jnp.max(jnp.maximum(run_scr[0], run_scr[1]))
        return t + 1, worst > -EXP_UNDERFLOW

    lax.while_loop(cond, body, (jnp.int32(1), jnp.bool_(True)))

    for hh in range(N_HEADS):
        g, j = divmod(hh, GROUP)
        o_ref[:, hh * LANES:(hh + 1) * LANES] = acc_scr[g, j * Q_BLOCK:(j + 1) * Q_BLOCK, :].astype(BF16)


def sb_attention(q, kv, batch):
    n = q.shape[0]
    t_pad = n // batch
    nq = t_pad // Q_BLOCK
    blk = lambda b, i: (b * nq + i, 0)
    return pl.pallas_call(
        _sb_body,
        grid=(batch, nq),
        in_specs=[pl.BlockSpec((Q_BLOCK, A_Q), blk),
                  pl.BlockSpec((t_pad, A_KV), lambda b, i: (b, 0)),
                  pl.BlockSpec((t_pad, A_KV), lambda b, i: (b, 1))],
        out_specs=pl.BlockSpec((Q_BLOCK, A_Q), blk),
        out_shape=jax.ShapeDtypeStruct((n, A_Q), BF16),
        scratch_shapes=[pltpu.VMEM((N_KV_HEADS, GROWS, LANES), BF16),
                        pltpu.VMEM((N_KV_HEADS, GROWS, LANES), F32),
                        pltpu.VMEM((N_KV_HEADS, GROWS, LANES), F32)],
        compiler_params=_params(("parallel", "arbitrary")),
        name="sb_attention",
    )(q, kv, kv)


def _extract_top(s, kidx, pidx):
    cur = s
    rank = jnp.full(s.shape, float(PEER_TOPK), F32)
    tops = jnp.zeros((PEER_TOPK, s.shape[1]), F32)
    for r in range(PEER_TOPK):
        m = jnp.max(cur, axis=0, keepdims=True)
        first = jnp.min(jnp.where(cur == m, kidx, float(N_KEYS)), axis=0, keepdims=True)
        hit = kidx == first
        rank = jnp.where(hit, float(r), rank)
        cur = jnp.where(hit, -jnp.inf, cur)
        tops = jnp.where(pidx == float(r), m, tops)
    return rank, tops


def _pair_words(x):
    u = pltpu.bitcast(x.astype(BF16).astype(F32), U32)
    return u | (u >> 16)


def _route_body(q_ref, sk_ref, e1_ref, lim_ref, e2_ref, r2_ref):
    kidx = lax.broadcasted_iota(I32, (N_KEYS, LANES), 0).astype(F32)
    pidx = lax.broadcasted_iota(I32, (PEER_TOPK, LANES), 0).astype(F32)

    def head(hh, carry):
        s1 = lax.dot_general(sk_ref[0], q_ref[2 * hh], NT_DIMS, preferred_element_type=F32)
        s2 = lax.dot_general(sk_ref[1], q_ref[2 * hh + 1], NT_DIMS, preferred_element_type=F32)
        rank1, a = _extract_top(s1, kidx, pidx)
        rank2, b = _extract_top(s2, kidx, pidx)

        def take(_, st):
            qmax, head_val = st
            m = jnp.max(head_val, axis=0, keepdims=True)
            first = jnp.min(jnp.where(head_val == m, pidx, float(PEER_TOPK)), axis=0, keepdims=True)
            hit = pidx == first
            qmax = jnp.where(hit, qmax + 1.0, qmax)
            qsel = jnp.sum(jnp.where(hit, qmax, 0.0), axis=0, keepdims=True)
            bnext = jnp.sum(jnp.where(pidx == qsel, b, 0.0), axis=0, keepdims=True)
            bnext = jnp.where(qsel >= float(PEER_TOPK), -jnp.inf, bnext)
            return qmax, jnp.where(hit, a + bnext, head_val)

        qmax, _ = lax.fori_loop(0, PEER_TOPK, take,
                                (jnp.zeros((PEER_TOPK, LANES), F32), a + b[0:1, :]),
                                unroll=True)

        ea = jnp.exp(a - a[0:1, :])
        eb = jnp.exp(b - b[0:1, :])
        pref = jnp.zeros((PEER_TOPK, LANES), F32)
        run = jnp.zeros((1, LANES), F32)
        for q in range(PEER_TOPK):
            run = run + eb[q:q + 1, :]
            pref = jnp.where(qmax == float(q + 1), run, pref)
        z = jnp.sum(ea * pref, axis=0, keepdims=True)

        lim = jnp.zeros((N_KEYS, LANES), F32)
        for p in range(PEER_TOPK):
            lim = jnp.where(rank1 == float(p), qmax[p:p + 1, :], lim)
        e1 = jnp.where(rank1 < float(PEER_TOPK), jnp.exp(s1 - a[0:1, :]), 0.0) / z
        e1_ref[hh] = _pair_words(e1)
        lim_ref[hh] = _pair_words(lim)
        e2_ref[hh] = jnp.where(rank2 < float(PEER_TOPK), jnp.exp(s2 - b[0:1, :]), 0.0).astype(BF16)
        r2_ref[hh] = rank2.astype(BF16)
        return carry

    def head_pair(hp, carry):
        head(2 * hp, carry)
        return head(2 * hp + 1, carry)

    lax.fori_loop(0, PEER_HEADS // 2, head_pair, 0)


def peer_route(q, subkeys):
    nhc, n, _ = q.shape
    tab = pl.BlockSpec((PEER_HEADS, N_KEYS, LANES), lambda t: (0, 0, t))
    words = jax.ShapeDtypeStruct((PEER_HEADS, N_KEYS, n), U32)
    halves = jax.ShapeDtypeStruct((PEER_HEADS, N_KEYS, n), BF16)
    return pl.pallas_call(
        _route_body,
        grid=(n // LANES,),
        in_specs=[pl.BlockSpec((nhc, LANES, LANES), lambda t: (0, t, 0)),
                  pl.BlockSpec((2, N_KEYS, PEER_HALF), lambda t: (0, 0, 0))],
        out_specs=[tab, tab, tab, tab],
        out_shape=[words, words, halves, halves],
        compiler_params=_params(("parallel",)),
        name="peer_route",
    )(q, subkeys)


def _gelu(x):
    return 0.5 * x * (1.0 + lax.erf(x * (2.0 ** -0.5)))


def _expert_body(h_ref, g_ref, u_ref, v_ref, e1_ref, lim_ref, e2_ref, r2_ref, out_ref,
                 xt_scr, acc_scr, ht_scr, pt_scr, e2_scr, r2_scr, *, tm, eb, ne):
    e = pl.program_id(1)

    def mm1(slab):
        ht_scr[slab % 2] = jnp.dot(u_ref[...], xt_scr[...], preferred_element_type=F32)

    def mm2(slab):
        acc_scr[...] += lax.dot_general(pt_scr[slab % 2], v_ref[...], (((0,), (0,)), ((), ())),
                                        preferred_element_type=F32)

    def gate(slab):
        par = slab % 2
        for ii in range(eb // N_KEYS):
            i = slab * (eb // N_KEYS) + ii
            e1_i = e1_ref[i]
            lim_i = lim_ref[i]
            for tc in range(tm // LANES):
                sl = slice(tc * LANES, (tc + 1) * LANES)
                g = jnp.zeros((N_KEYS, LANES), BF16)
                for hh in range(PEER_HEADS):
                    e1b = pltpu.bitcast(jnp.broadcast_to(e1_i[hh:hh + 1, sl], (N_KEYS // 2, LANES)), BF16)
                    limb = pltpu.bitcast(jnp.broadcast_to(lim_i[hh:hh + 1, sl], (N_KEYS // 2, LANES)), BF16)
                    picked = jnp.where(r2_scr[hh, :, sl] < limb, e2_scr[hh, :, sl], jnp.zeros((), BF16))
                    g = g + picked * e1b
                act = _gelu(ht_scr[par, ii * N_KEYS:(ii + 1) * N_KEYS, sl])
                pt_scr[par, ii * N_KEYS:(ii + 1) * N_KEYS, sl] = act.astype(BF16) * g

    @pl.when(e == 0)
    def _():
        y = _rms(h_ref[...], g_ref[...])
        xt_scr[...] = y.T.astype(BF16)
        acc_scr[...] = jnp.zeros_like(acc_scr)
        e2_scr[...] = e2_ref[...]
        r2_scr[...] = r2_ref[...]
        mm1(e)

    @pl.when(e == 1)
    def _():
        mm1(e)
        gate(e - 1)

    @pl.when(jnp.logical_and(e >= 2, e < ne))
    def _():
        mm1(e)
        mm2(e - 2)
        gate(e - 1)

    @pl.when(e == ne)
    def _():
        mm2(e - 2)
        gate(e - 1)

    @pl.when(e == ne + 1)
    def _():
        mm2(e - 2)
        out_ref[...] = h_ref[...] + acc_scr[...]


def peer_experts(h, gain, u, v, tabs, tm, eb):
    n, d = h.shape
    ne = u.shape[0] // eb
    tab = pl.BlockSpec((PEER_HEADS, N_KEYS, tm), lambda t, e: (0, 0, t))
    rtab = pl.BlockSpec((N_KEYS, PEER_HEADS, tm), lambda t, e: (0, 0, t))
    e1, lim, e2, r2 = tabs
    tabs = (jnp.transpose(e1, (1, 0, 2)), jnp.transpose(lim, (1, 0, 2)), e2, r2)
    return pl.pallas_call(
        functools.partial(_expert_body, tm=tm, eb=eb, ne=ne),
        grid=(n // tm, ne + 2),
        in_specs=[pl.BlockSpec((tm, d), lambda t, e: (t, 0)),
                  pl.BlockSpec((1, d), lambda t, e: (0, 0)),
                  pl.BlockSpec((eb, d), lambda t, e: (jnp.minimum(e, ne - 1), 0)),
                  pl.BlockSpec((eb, d), lambda t, e: (jnp.clip(e - 2, 0, ne - 1), 0)),
                  rtab, rtab, tab, tab],
        out_specs=pl.BlockSpec((tm, d), lambda t, e: (t, 0)),
        out_shape=jax.ShapeDtypeStruct((n, d), F32),
        scratch_shapes=[pltpu.VMEM((d, tm), BF16), pltpu.VMEM((tm, d), F32),
                        pltpu.VMEM((2, eb, tm), F32), pltpu.VMEM((2, eb, tm), BF16),
                        pltpu.VMEM((PEER_HEADS, N_KEYS, tm), BF16),
                        pltpu.VMEM((PEER_HEADS, N_KEYS, tm), BF16)],
        compiler_params=_params(("parallel", "arbitrary")),
        name="peer_experts",
    )(h, gain.reshape(1, d), u, v, *tabs)


def peer_ffn(h, gain, w_q, subkeys, u, v, tm_q, tm_e, eb):
    q = peer_query(h, gain, w_q, tm_q)
    tabs = peer_route(q, subkeys)
    return peer_experts(h, gain, u, v, tabs, tm_e, eb)


def _rope_tables(t_pad, dim, period):
    r = dim // 4
    half = r // 2
    pos = jnp.arange(t_pad, dtype=F32)
    inv = ROPE_THETA ** (-jnp.arange(half, dtype=F32) * 2.0 / r)
    ang = pos[:, None] * inv[None, :]
    cos, sin = jnp.cos(ang), jnp.sin(ang)
    pad = period - 2 * half
    c = jnp.concatenate([cos, cos, jnp.ones((t_pad, pad), F32)], axis=1)
    sa = jnp.concatenate([-sin, jnp.zeros((t_pad, half + pad), F32)], axis=1)
    sb = jnp.concatenate([jnp.zeros((t_pad, half), F32), sin, jnp.zeros((t_pad, pad), F32)], axis=1)
    reps = LANES // period
    return tuple(jnp.tile(a, (1, reps)) for a in (c, sa, sb))


def _a_weights(w_in):
    d = w_in.shape[0]
    lo = A_Q + 2 * A_KV
    zeros = jnp.zeros((d, IDX_DIM), w_in.dtype)
    cols = [w_in[:, :lo]]
    for hh in range(IDX_HEADS):
        qi = w_in[:, lo + hh * IDX_DIM:lo + (hh + 1) * IDX_DIM]
        cols += [qi, qi, qi, zeros]
    lo += IDX_HEADS * IDX_DIM
    ki = w_in[:, lo:lo + IDX_DIM]
    cols += [ki, ki, ki, zeros]
    lo += IDX_DIM
    cols += [w_in[:, lo:lo + IDX_HEADS], jnp.zeros((d, LANES - IDX_HEADS), w_in.dtype)]
    return jnp.concatenate(cols, axis=1).astype(BF16)


def kernel(x, meta_tokens, a_norm, a_w_in, a_q_norm, a_k_norm, a_w_o, kv_norm, kv_w,
           b_norm, b_w_q, b_w_o, ffn_norm, peer_w_q, peer_subkeys, peer_u, peer_v):
    b, s, d = x.shape
    t = s + N_META
    t_pad = -(-t // Q_BLOCK) * Q_BLOCK
    n = b * t_pad
    depth = ffn_norm.shape[0]
    n_a = a_norm.shape[0]
    topk = min(INDEX_TOPK, s // 4)

    h = jnp.concatenate([jnp.broadcast_to(meta_tokens[None].astype(x.dtype), (b, N_META, d)), x], axis=1)
    h = jnp.pad(h, ((0, 0), (0, t_pad - t), (0, 0))).reshape(n, d)

    tabs = _rope_tables(t_pad, HEAD_DIM, LANES) + _rope_tables(t_pad, IDX_DIM, IDX_DIM)
    tm_a = t_pad // 11 if t_pad % 11 == 0 and (t_pad // 11) % 8 == 0 else Q_BLOCK
    tm = 512 if n % 512 == 0 else Q_BLOCK

    kv = None
    for layer in range(depth):
        if layer < n_a:
            q, k, v, ai, kb, wi = a_proj(h, a_norm[layer], _a_weights(a_w_in[layer]),
                                         a_q_norm[layer], a_k_norm[layer], tabs, b, tm_a)
            o = dsa_attention(q, k, v, ai, kb, wi, b, topk)
            h = out_proj(h, o, a_w_o[layer].astype(BF16), tm)
        else:
            j = layer - n_a
            q = rms_proj(h, b_norm[j], b_w_q[j].astype(BF16), tm, BF16)
            o = sb_attention(q, kv, b)
            h = out_proj(h, o, b_w_o[j].astype(BF16), tm)
        h = peer_ffn(h, ffn_norm[layer], peer_w_q[layer].astype(BF16),
                     peer_subkeys[layer].astype(BF16), peer_u[layer].astype(BF16),
                     peer_v[layer].astype(BF16), tm, tm, 4 * N_KEYS)
        if layer == n_a - 1:
            kv = rms_proj(h, kv_norm, kv_w.astype(BF16), tm, BF16)
    return h.reshape(b, t_pad, d)[:, N_META:N_META + s]
```

```python
import functools

import jax
import jax.numpy as jnp
from jax import lax
from jax.experimental import pallas as pl
from jax.experimental.pallas import tpu as pltpu

F32 = jnp.float32
BF16 = jnp.bfloat16
I32 = jnp.int32
U32 = jnp.uint32

D_MODEL = 1024
N_META = 16
N_HEADS = 8
HEAD_DIM = 128
N_KV_HEADS = 2
GROUP = N_HEADS // N_KV_HEADS
ROPE_THETA = 500000.0
IDX_HEADS = 4
IDX_DIM = 64
INDEX_TOPK = 256
Q_BLOCK = 128
PEER_HEADS = 8
PEER_HALF = 128
N_KEYS = 128
PEER_TOPK = 16
RMS_EPS = 1e-6

LANES = 128
KCH = 2 * LANES
INT_MIN = -(2 ** 31)
NEG_BIG = -1e30
EXP_UNDERFLOW = 104.0
VMEM_LIMIT = 48 * 1024 * 1024

A_Q = N_HEADS * HEAD_DIM
A_KV = N_KV_HEADS * HEAD_DIM
A_AI = IDX_HEADS * KCH
A_KB = KCH
GROWS = GROUP * Q_BLOCK
SCALE = HEAD_DIM ** -0.5
NT_DIMS = (((1,), (1,)), ((), ()))


def _params(sem):
    return pltpu.CompilerParams(dimension_semantics=sem, vmem_limit_bytes=VMEM_LIMIT)


def _rms(x, gain):
    return x * lax.rsqrt(jnp.mean(x * x, axis=-1, keepdims=True) + RMS_EPS) * gain


def _rms_proj_body(h_ref, g_ref, w_ref, o_ref):
    y = _rms(h_ref[...], g_ref[...]).astype(BF16)
    o_ref[...] = jnp.dot(y, w_ref[...], preferred_element_type=F32).astype(o_ref.dtype)


def rms_proj(h, gain, w, tm, out_dtype):
    n, d = h.shape
    nout = w.shape[1]
    return pl.pallas_call(
        _rms_proj_body,
        grid=(n // tm,),
        in_specs=[pl.BlockSpec((tm, d), lambda i: (i, 0)),
                  pl.BlockSpec((1, d), lambda i: (0, 0)),
                  pl.BlockSpec((d, nout), lambda i: (0, 0))],
        out_specs=pl.BlockSpec((tm, nout), lambda i: (i, 0)),
        out_shape=jax.ShapeDtypeStruct((n, nout), out_dtype),
        compiler_params=_params(("parallel",)),
        name="rms_proj",
    )(h, gain.reshape(1, d), w)


def _peer_q_body(h_ref, g_ref, w_ref, o_ref):
    y = _rms(h_ref[...], g_ref[...]).astype(BF16)
    for hc in range(2 * PEER_HEADS):
        o_ref[hc] = jnp.dot(y, w_ref[:, hc * LANES:(hc + 1) * LANES],
                            preferred_element_type=F32).astype(o_ref.dtype)


def peer_query(h, gain, w, tm):
    n, d = h.shape
    nhc = 2 * PEER_HEADS
    return pl.pallas_call(
        _peer_q_body,
        grid=(n // tm,),
        in_specs=[pl.BlockSpec((tm, d), lambda i: (i, 0)),
                  pl.BlockSpec((1, d), lambda i: (0, 0)),
                  pl.BlockSpec((d, nhc * LANES), lambda i: (0, 0))],
        out_specs=pl.BlockSpec((nhc, tm, LANES), lambda i: (0, i, 0)),
        out_shape=jax.ShapeDtypeStruct((nhc, n, LANES), BF16),
        compiler_params=_params(("parallel",)),
        name="peer_query",
    )(h, gain.reshape(1, d), w)


def _out_proj_body(h_ref, o_ref, w_ref, out_ref):
    out_ref[...] = h_ref[...] + jnp.dot(o_ref[...], w_ref[...], preferred_element_type=F32)


def out_proj(h, o, w, tm):
    n, d = h.shape
    k = o.shape[1]
    return pl.pallas_call(
        _out_proj_body,
        grid=(n // tm,),
        in_specs=[pl.BlockSpec((tm, d), lambda i: (i, 0)),
                  pl.BlockSpec((tm, k), lambda i: (i, 0)),
                  pl.BlockSpec((k, d), lambda i: (0, 0))],
        out_specs=pl.BlockSpec((tm, d), lambda i: (i, 0)),
        out_shape=jax.ShapeDtypeStruct((n, d), F32),
        compiler_params=_params(("parallel",)),
        name="out_proj",
    )(h, o, w)


def _rope(x, c, sa, sb, half):
    return x * c + pltpu.roll(x, LANES - half, 1) * sa + pltpu.roll(x, half, 1) * sb


def _a_proj_body(h_ref, g_ref, w_ref, qg_ref, kg_ref, hc_ref, hsa_ref, hsb_ref,
                 ic_ref, isa_ref, isb_ref, q_ref, k_ref, v_ref, ai_ref, kb_ref, wi_ref):
    y = _rms(h_ref[...], g_ref[...]).astype(BF16)
    hc, hsa, hsb = hc_ref[...], hsa_ref[...], hsb_ref[...]
    ic, isa, isb = ic_ref[...], isa_ref[...], isb_ref[...]
    for hh in range(N_HEADS + N_KV_HEADS):
        lo = hh * LANES
        p = jnp.dot(y, w_ref[:, lo:lo + LANES], preferred_element_type=F32)
        gain = qg_ref[...] if hh < N_HEADS else kg_ref[...]
        p = _rope(_rms(p, gain), hc, hsa, hsb, HEAD_DIM // 8)
        if hh < N_HEADS:
            q_ref[:, lo:lo + LANES] = p.astype(BF16)
        else:
            k_ref[:, lo - A_Q:lo - A_Q + LANES] = p.astype(BF16)
    lo = A_Q + A_KV
    v_ref[...] = jnp.dot(y, w_ref[:, lo:lo + A_KV], preferred_element_type=F32).astype(BF16)
    lo += A_KV
    first_group = lax.broadcasted_iota(I32, (y.shape[0], LANES), 1) < IDX_DIM
    n_q_slabs = A_AI // LANES
    for s in range(n_q_slabs + A_KB // LANES):
        p = jnp.dot(y, w_ref[:, lo + s * LANES:lo + (s + 1) * LANES], preferred_element_type=F32)
        p = _rope(p, ic, isa, isb, IDX_DIM // 8)
        if s < n_q_slabs:
            p = p * (IDX_DIM ** -0.5)
        hi = p.astype(BF16).astype(F32)
        if s < n_q_slabs:
            val = jnp.where(first_group, hi, p - hi) if s % 2 == 0 else hi
            ai_ref[:, s * LANES:(s + 1) * LANES] = val.astype(BF16)
        else:
            val = hi if s == n_q_slabs else p - hi
            kb_ref[:, (s - n_q_slabs) * LANES:(s - n_q_slabs + 1) * LANES] = val.astype(BF16)
    lo += A_AI + A_KB
    wi_ref[...] = jnp.dot(y, w_ref[:, lo:lo + LANES], preferred_element_type=F32) * (IDX_HEADS ** -0.5)


def a_proj(h, gain, w_all, q_gain, k_gain, tabs, batch, tm):
    n, d = h.shape
    t_pad = n // batch
    nt = t_pad // tm
    row = lambda b, j: (b * nt + j, 0)
    tab = pl.BlockSpec((tm, LANES), lambda b, j: (j, 0))
    const = lambda shape: pl.BlockSpec(shape, lambda b, j: (0, 0))
    widths = (A_Q, A_KV, A_KV, A_AI, A_KB)
    return pl.pallas_call(
        _a_proj_body,
        grid=(batch, nt),
        in_specs=[pl.BlockSpec((tm, d), row), const((1, d)), const(w_all.shape),
                  const((1, LANES)), const((1, LANES)), tab, tab, tab, tab, tab, tab],
        out_specs=[pl.BlockSpec((tm, w), row) for w in widths] + [pl.BlockSpec((tm, LANES), row)],
        out_shape=[jax.ShapeDtypeStruct((n, w), BF16) for w in widths]
        + [jax.ShapeDtypeStruct((n, LANES), F32)],
        compiler_params=_params(("parallel", "parallel")),
        name="a_proj",
    )(h, gain.reshape(1, d), w_all, q_gain.reshape(1, LANES), k_gain.reshape(1, LANES), *tabs)


def _dsa_body(q_ref, ai_ref, wi_ref, k_ref, ve_ref, kb_ref, o_ref,
              key_scr, bias_scr, wib_scr, qs_scr, mx_scr, acc_scr, *, topk):
    i = pl.program_id(1)
    nch = i + 1
    nch2 = (i + 2) // 2
    row2 = lax.broadcasted_iota(I32, (Q_BLOCK, KCH), 0)
    col2 = lax.broadcasted_iota(I32, (Q_BLOCK, KCH), 1)

    for hh in range(IDX_HEADS):
        wib_scr[hh] = jnp.broadcast_to(wi_ref[:, hh:hh + 1], (Q_BLOCK, KCH))
    for hh in range(N_HEADS):
        g, j = divmod(hh, GROUP)
        qs_scr[g, j * Q_BLOCK:(j + 1) * Q_BLOCK, :] = q_ref[:, hh * LANES:(hh + 1) * LANES]

    def score_chunk(c, carry):
        base = pl.multiple_of(c * KCH, KCH)
        kc = kb_ref[pl.ds(base, KCH), :]
        s = jnp.zeros((Q_BLOCK, KCH), F32)
        for hh in range(IDX_HEADS):
            d = lax.dot_general(ai_ref[:, hh * KCH:(hh + 1) * KCH], kc, NT_DIMS,
                                preferred_element_type=F32)
            s = s + jnp.maximum(d, 0.0) * wib_scr[hh]
        s = jnp.where(s == 0.0, 0.0, s)
        bits = pltpu.bitcast(s, I32)
        key = jnp.where(bits < 0, bits ^ jnp.int32(0x7FFFFFFF), bits)
        causal = (c * KCH + col2) <= (i * Q_BLOCK + row2)
        key = jnp.where(causal, key, jnp.int32(INT_MIN))
        key_scr[2 * c] = key[:, :LANES]
        key_scr[2 * c + 1] = key[:, LANES:]
        return carry

    lax.fori_loop(0, nch2, score_chunk, 0)

    def count(pred_fn):
        def body(c, cnt):
            return cnt + jnp.where(pred_fn(key_scr[c]), 1.0, 0.0)
        cnt = lax.fori_loop(0, nch, body, jnp.zeros((Q_BLOCK, LANES), F32))
        return jnp.sum(cnt, axis=1, keepdims=True)

    def bs_pass(b, ans):
        cand = ans + lax.shift_left(jnp.int32(1), jnp.int32(31) - b)
        candb = jnp.broadcast_to(cand, (Q_BLOCK, LANES))
        tot = count(lambda key: key >= candb)
        return jnp.where(tot >= float(topk), cand, ans)

    ans = lax.fori_loop(0, 32, bs_pass, jnp.full((Q_BLOCK, 1), INT_MIN, I32))
    ansb = jnp.broadcast_to(ans, (Q_BLOCK, LANES))
    n_gt = count(lambda key: key > ansb)
    need = jnp.where(ans == jnp.int32(INT_MIN), 0.0, float(topk) - n_gt)
    needb = jnp.broadcast_to(need, (Q_BLOCK, LANES))

    r2 = lax.broadcasted_iota(I32, (LANES, 2 * LANES), 0)
    c2 = lax.broadcasted_iota(I32, (LANES, 2 * LANES), 1)
    pre_tot = jnp.where((r2 < c2) | (c2 >= LANES), 1.0, 0.0).astype(BF16)

    def bias_chunk(c, ties_seen):
        key = key_scr[c]
        eq = key == ansb
        pt = jnp.dot(jnp.where(eq, 1.0, 0.0).astype(BF16), pre_tot, preferred_element_type=F32)
        tie_ok = (ties_seen + pt[:, :LANES]) < needb
        bias_scr[c] = jnp.where(key > ansb, 0.0,
                                jnp.where(eq, jnp.where(tie_ok, 0.0, NEG_BIG), NEG_BIG))
        return ties_seen + pt[:, LANES:]

    lax.fori_loop(0, nch, bias_chunk, jnp.zeros((Q_BLOCK, LANES), F32))
    bias_scr[nch] = jnp.full((Q_BLOCK, LANES), NEG_BIG, F32)

    def logits(c, g, bias4):
        base = pl.multiple_of(c * KCH, KCH)
        s = lax.dot_general(qs_scr[g], k_ref[pl.ds(base, KCH), g * LANES:(g + 1) * LANES],
                            NT_DIMS, preferred_element_type=F32)
        return s * SCALE + bias4

    def bias_rows(c):
        b2 = jnp.concatenate([bias_scr[2 * c], bias_scr[2 * c + 1]], axis=1)
        return jnp.tile(b2, (GROUP, 1))

    mx_scr[...] = jnp.full(mx_scr.shape, NEG_BIG, F32)

    def max_chunk(c, carry):
        bias4 = bias_rows(c)
        for g in range(N_KV_HEADS):
            s = logits(c, g, bias4)
            mx_scr[g] = jnp.maximum(mx_scr[g], jnp.maximum(s[:, :LANES], s[:, LANES:]))
        return carry

    lax.fori_loop(0, nch2, max_chunk, 0)
    for g in range(N_KV_HEADS):
        mx_scr[g] = jnp.broadcast_to(jnp.max(mx_scr[g], axis=1, keepdims=True), (GROWS, LANES))

    acc_scr[...] = jnp.zeros_like(acc_scr)

    def acc_chunk(c, carry):
        bias4 = bias_rows(c)
        base = pl.multiple_of(c * KCH, KCH)
        for g in range(N_KV_HEADS):
            m = mx_scr[g]
            p = jnp.exp(logits(c, g, bias4) - jnp.concatenate([m, m], axis=1))
            acc_scr[g] += jnp.dot(p.astype(BF16), ve_ref[pl.ds(base, KCH), g * KCH:(g + 1) * KCH],
                                  preferred_element_type=F32)
        return carry

    lax.fori_loop(0, nch2, acc_chunk, 0)

    for hh in range(N_HEADS):
        g, j = divmod(hh, GROUP)
        a = acc_scr[g, j * Q_BLOCK:(j + 1) * Q_BLOCK, :]
        o_ref[:, hh * LANES:(hh + 1) * LANES] = (a[:, :LANES] / a[:, LANES:]).astype(BF16)


def _pad_keys(a, batch, tk):
    t_pad = a.shape[0] // batch
    a = a.reshape(batch, t_pad, a.shape[1])
    return jnp.pad(a, ((0, 0), (0, tk - t_pad), (0, 0))).reshape(batch * tk, a.shape[2])


def dsa_attention(q, k, v, ai, kb, wi, batch, topk):
    n = q.shape[0]
    t_pad = n // batch
    nq = t_pad // Q_BLOCK
    tk = ((nq + 1) // 2) * KCH
    ones = jnp.ones((n, LANES), v.dtype)
    ve = jnp.concatenate([v[:, :LANES], ones, v[:, LANES:], ones], axis=1)
    k, ve, kb = _pad_keys(k, batch, tk), _pad_keys(ve, batch, tk), _pad_keys(kb, batch, tk)
    blk = lambda b, i: (b * nq + i, 0)
    full = lambda b, i: (b, 0)
    return pl.pallas_call(
        functools.partial(_dsa_body, topk=topk),
        grid=(batch, nq),
        in_specs=[pl.BlockSpec((Q_BLOCK, A_Q), blk), pl.BlockSpec((Q_BLOCK, A_AI), blk),
                  pl.BlockSpec((Q_BLOCK, LANES), blk),
                  pl.BlockSpec((tk, A_KV), full), pl.BlockSpec((tk, 2 * KCH), full),
                  pl.BlockSpec((tk, A_KB), full)],
        out_specs=pl.BlockSpec((Q_BLOCK, A_Q), blk),
        out_shape=jax.ShapeDtypeStruct((n, A_Q), BF16),
        scratch_shapes=[pltpu.VMEM((nq + 2, Q_BLOCK, LANES), I32),
                        pltpu.VMEM((nq + 2, Q_BLOCK, LANES), F32),
                        pltpu.VMEM((IDX_HEADS, Q_BLOCK, KCH), F32),
                        pltpu.VMEM((N_KV_HEADS, GROWS, LANES), BF16),
                        pltpu.VMEM((N_KV_HEADS, GROWS, LANES), F32),
                        pltpu.VMEM((N_KV_HEADS, GROWS, KCH), F32)],
        compiler_params=_params(("parallel", "arbitrary")),
        name="dsa_attention",
    )(q, ai, wi, k, ve, kb)


def _sb_body(q_ref, k_ref, v_ref, o_ref, qs_scr, run_scr, acc_scr):
    i = pl.program_id(1)
    for hh in range(N_HEADS):
        g, j = divmod(hh, GROUP)
        qs_scr[g, j * Q_BLOCK:(j + 1) * Q_BLOCK, :] = q_ref[:, hh * LANES:(hh + 1) * LANES]
    run_scr[...] = jnp.zeros_like(run_scr)
    acc_scr[...] = jnp.zeros_like(acc_scr)

    r2 = lax.broadcasted_iota(I32, (2 * LANES, 2 * LANES), 0) % LANES
    c2 = lax.broadcasted_iota(I32, (2 * LANES, 2 * LANES), 1)
    suf_tot = jnp.where((r2 > c2) | (c2 >= LANES), 1.0, 0.0).astype(BF16)

    def step(c, causal):
        base = pl.multiple_of(c * LANES, LANES)
        for g in range(N_KV_HEADS):
            z = lax.dot_general(qs_scr[g], k_ref[pl.ds(base, LANES), g * LANES:(g + 1) * LANES],
                                NT_DIMS, preferred_element_type=F32) * SCALE
            sp = jnp.maximum(z, 0.0) + jnp.log1p(jnp.exp(-jnp.abs(z)))
            lk = -sp if causal is None else jnp.where(causal, -sp, 0.0)
            hi = lk.astype(BF16)
            lo = (lk - hi.astype(F32)).astype(BF16)
            st = jnp.dot(jnp.concatenate([hi, lo], axis=1), suf_tot, preferred_element_type=F32)
            a = jnp.exp(z - sp + st[:, :LANES] + run_scr[g])
            if causal is not None:
                a = jnp.where(causal, a, 0.0)
            acc_scr[g] += jnp.dot(a.astype(BF16), v_ref[pl.ds(base, LANES), g * LANES:(g + 1) * LANES],
                                  preferred_element_type=F32)
            run_scr[g] += st[:, LANES:]

    row = lax.broadcasted_iota(I32, (GROWS, LANES), 0) % Q_BLOCK
    col = lax.broadcasted_iota(I32, (GROWS, LANES), 1)
    step(i, col < row)

    def cond(state):
        t, live = state
        return jnp.logical_and(t <= i, live)

    def body(state):
        t, _ = state
        step(i - t, None)
        worst = jnp.max(jnp.maximum(run_scr[0], run_scr[1]))
        return t + 1, worst > -EXP_UNDERFLOW

    lax.while_loop(cond, body, (jnp.int32(1), jnp.bool_(True)))

    for hh in range(N_HEADS):
        g, j = divmod(hh, GROUP)
        o_ref[:, hh * LANES:(hh + 1) * LANES] = acc_scr[g, j * Q_BLOCK:(j + 1) * Q_BLOCK, :].astype(BF16)


def sb_attention(q, kv, batch):
    n = q.shape[0]
    t_pad = n // batch
    nq = t_pad // Q_BLOCK
    blk = lambda b, i: (b * nq + i, 0)
    return pl.pallas_call(
        _sb_body,
        grid=(batch, nq),
        in_specs=[pl.BlockSpec((Q_BLOCK, A_Q), blk),
                  pl.BlockSpec((t_pad, A_KV), lambda b, i: (b, 0)),
                  pl.BlockSpec((t_pad, A_KV), lambda b, i: (b, 1))],
        out_specs=pl.BlockSpec((Q_BLOCK, A_Q), blk),
        out_shape=jax.ShapeDtypeStruct((n, A_Q), BF16),
        scratch_shapes=[pltpu.VMEM((N_KV_HEADS, GROWS, LANES), BF16),
                        pltpu.VMEM((N_KV_HEADS, GROWS, LANES), F32),
                        pltpu.VMEM((N_KV_HEADS, GROWS, LANES), F32)],
        compiler_params=_params(("parallel", "arbitrary")),
        name="sb_attention",
    )(q, kv, kv)


def _extract_top(s, kidx, pidx):
    cur = s
    rank = jnp.full(s.shape, float(PEER_TOPK), F32)
    tops = jnp.zeros((PEER_TOPK, s.shape[1]), F32)
    for r in range(PEER_TOPK):
        m = jnp.max(cur, axis=0, keepdims=True)
        first = jnp.min(jnp.where(cur == m, kidx, float(N_KEYS)), axis=0, keepdims=True)
        hit = kidx == first
        rank = jnp.where(hit, float(r), rank)
        cur = jnp.where(hit, -jnp.inf, cur)
        tops = jnp.where(pidx == float(r), m, tops)
    return rank, tops


def _bf16_bits(x):
    return pltpu.bitcast(x.astype(BF16).astype(F32), U32)


def _route_body(q_ref, sk_ref, el_ref, e2_ref, r2_ref):
    kidx = lax.broadcasted_iota(I32, (N_KEYS, LANES), 0).astype(F32)
    pidx = lax.broadcasted_iota(I32, (PEER_TOPK, LANES), 0).astype(F32)

    def head(hh, carry):
        s1 = lax.dot_general(sk_ref[0], q_ref[2 * hh], NT_DIMS, preferred_element_type=F32)
        s2 = lax.dot_general(sk_ref[1], q_ref[2 * hh + 1], NT_DIMS, preferred_element_type=F32)
        rank1, a = _extract_top(s1, kidx, pidx)
        rank2, b = _extract_top(s2, kidx, pidx)

        def take(_, st):
            qmax, head_val = st
            m = jnp.max(head_val, axis=0, keepdims=True)
            first = jnp.min(jnp.where(head_val == m, pidx, float(PEER_TOPK)), axis=0, keepdims=True)
            hit = pidx == first
            qmax = jnp.where(hit, qmax + 1.0, qmax)
            qsel = jnp.sum(jnp.where(hit, qmax, 0.0), axis=0, keepdims=True)
            bnext = jnp.sum(jnp.where(pidx == qsel, b, 0.0), axis=0, keepdims=True)
            bnext = jnp.where(qsel >= float(PEER_TOPK), -jnp.inf, bnext)
            return qmax, jnp.where(hit, a + bnext, head_val)

        qmax, _ = lax.fori_loop(0, PEER_TOPK, take,
                                (jnp.zeros((PEER_TOPK, LANES), F32), a + b[0:1, :]),
                                unroll=True)

        ea = jnp.exp(a - a[0:1, :])
        eb = jnp.exp(b - b[0:1, :])
        pref = jnp.zeros((PEER_TOPK, LANES), F32)
        run = jnp.zeros((1, LANES), F32)
        for q in range(PEER_TOPK):
            run = run + eb[q:q + 1, :]
            pref = jnp.where(qmax == float(q + 1), run, pref)
        z = jnp.sum(ea * pref, axis=0, keepdims=True)

        lim = jnp.zeros((N_KEYS, LANES), F32)
        for p in range(PEER_TOPK):
            lim = jnp.where(rank1 == float(p), qmax[p:p + 1, :], lim)
        e1 = jnp.where(rank1 < float(PEER_TOPK), jnp.exp(s1 - a[0:1, :]), 0.0) / z
        el_ref[hh] = _bf16_bits(e1) | (_bf16_bits(lim) >> 16)
        e2_ref[hh] = jnp.where(rank2 < float(PEER_TOPK), jnp.exp(s2 - b[0:1, :]), 0.0)
        r2_ref[hh] = rank2
        return carry

    def head_pair(hp, carry):
        head(2 * hp, carry)
        return head(2 * hp + 1, carry)

    lax.fori_loop(0, PEER_HEADS // 2, head_pair, 0)


def peer_route(q, subkeys):
    nhc, n, _ = q.shape
    tab = pl.BlockSpec((PEER_HEADS, N_KEYS, LANES), lambda t: (0, 0, t))
    words = jax.ShapeDtypeStruct((PEER_HEADS, N_KEYS, n), U32)
    halves = jax.ShapeDtypeStruct((PEER_HEADS, N_KEYS, n), F32)
    return pl.pallas_call(
        _route_body,
        grid=(n // LANES,),
        in_specs=[pl.BlockSpec((nhc, LANES, LANES), lambda t: (0, t, 0)),
                  pl.BlockSpec((2, N_KEYS, PEER_HALF), lambda t: (0, 0, 0))],
        out_specs=[tab, tab, tab],
        out_shape=[words, halves, halves],
        compiler_params=_params(("parallel",)),
        name="peer_route",
    )(q, subkeys)


def _gelu(x):
    return 0.5 * x * (1.0 + lax.erf(x * (2.0 ** -0.5)))


def _expert_body(h_ref, g_ref, u_ref, v_ref, el_ref, e2_ref, r2_ref, out_ref,
                 xt_scr, acc_scr, ht_scr, pt_scr, *, tm, eb, ne):
    e = pl.program_id(1)
    nlc = tm // LANES
    ndc = acc_scr.shape[0]

    nii = eb // N_KEYS

    def mm1(slab):
        res = jnp.dot(u_ref[...], xt_scr[...], preferred_element_type=F32)
        for tc in range(nlc):
            ht_scr[slab % 2, tc] = res[:, tc * LANES:(tc + 1) * LANES]

    def mm2(slab):
        lhs = jnp.concatenate([pt_scr[slab % 2, tc] for tc in range(nlc)], axis=1)
        res = lax.dot_general(lhs, v_ref[...], (((0,), (0,)), ((), ())), preferred_element_type=F32)
        for dc in range(ndc):
            acc_scr[dc] += res[:, dc * LANES:(dc + 1) * LANES]

    def gate(slab):
        par = slab % 2
        for ii in range(nii):
            el_i = el_ref[slab * nii + ii]
            e1_i = pltpu.bitcast(el_i & jnp.uint32(0xFFFF0000), F32)
            lim_i = pltpu.bitcast(el_i << 16, F32)
            for tc in range(nlc):
                sl = slice(tc * LANES, (tc + 1) * LANES)
                g = jnp.zeros((N_KEYS, LANES), F32)
                for hh in range(PEER_HEADS):
                    picked = jnp.where(r2_ref[hh, :, sl] < lim_i[hh:hh + 1, sl], e2_ref[hh, :, sl], 0.0)
                    g = g + picked * e1_i[hh:hh + 1, sl]
                act = _gelu(ht_scr[par, tc, ii * N_KEYS:(ii + 1) * N_KEYS, :])
                pt_scr[par, tc, ii * N_KEYS:(ii + 1) * N_KEYS, :] = (act * g).astype(BF16)

    @pl.when(e == 0)
    def _():
        y = _rms(h_ref[...], g_ref[...])
        xt_scr[...] = y.T.astype(BF16)
        acc_scr[...] = jnp.zeros_like(acc_scr)
        mm1(e)

    @pl.when(e == 1)
    def _():
        mm1(e)
        gate(e - 1)

    @pl.when(jnp.logical_and(e >= 2, e < ne))
    def _():
        mm1(e)
        mm2(e - 2)
        gate(e - 1)

    @pl.when(e == ne)
    def _():
        mm2(e - 2)
        gate(e - 1)

    @pl.when(e == ne + 1)
    def _():
        mm2(e - 2)
        out_ref[...] = h_ref[...] + jnp.concatenate([acc_scr[dc] for dc in range(ndc)], axis=1)


def peer_experts(h, gain, u, v, tabs, tm, eb):
    n, d = h.shape
    ne = u.shape[0] // eb
    nlc = tm // LANES
    tab = pl.BlockSpec((PEER_HEADS, N_KEYS, tm), lambda t, e: (0, 0, t))
    rtab = pl.BlockSpec((N_KEYS, PEER_HEADS, tm), lambda t, e: (0, 0, t))
    el, e2, r2 = tabs
    el = jnp.transpose(el, (1, 0, 2))
    return pl.pallas_call(
        functools.partial(_expert_body, tm=tm, eb=eb, ne=ne),
        grid=(n // tm, ne + 2),
        in_specs=[pl.BlockSpec((tm, d), lambda t, e: (t, 0)),
                  pl.BlockSpec((1, d), lambda t, e: (0, 0)),
                  pl.BlockSpec((eb, d), lambda t, e: (jnp.minimum(e, ne - 1), 0)),
                  pl.BlockSpec((eb, d), lambda t, e: (jnp.clip(e - 2, 0, ne - 1), 0)),
                  rtab, tab, tab],
        out_specs=pl.BlockSpec((tm, d), lambda t, e: (t, 0)),
        out_shape=jax.ShapeDtypeStruct((n, d), F32),
        scratch_shapes=[pltpu.VMEM((d, tm), BF16), pltpu.VMEM((d // LANES, tm, LANES), F32),
                        pltpu.VMEM((2, nlc, eb, LANES), F32), pltpu.VMEM((2, nlc, eb, LANES), BF16)],
        compiler_params=_params(("parallel", "arbitrary")),
        name="peer_experts",
    )(h, gain.reshape(1, d), u, v, el, e2, r2)


def peer_ffn(h, gain, w_q, subkeys, u, v, tm_q, tm_e, eb):
    q = peer_query(h, gain, w_q, tm_q)
    tabs = peer_route(q, subkeys)
    return peer_experts(h, gain, u, v, tabs, tm_e, eb)


def _rope_tables(t_pad, dim, period):
    r = dim // 4
    half = r // 2
    pos = jnp.arange(t_pad, dtype=F32)
    inv = ROPE_THETA ** (-jnp.arange(half, dtype=F32) * 2.0 / r)
    ang = pos[:, None] * inv[None, :]
    cos, sin = jnp.cos(ang), jnp.sin(ang)
    pad = period - 2 * half
    c = jnp.concatenate([cos, cos, jnp.ones((t_pad, pad), F32)], axis=1)
    sa = jnp.concatenate([-sin, jnp.zeros((t_pad, half + pad), F32)], axis=1)
    sb = jnp.concatenate([jnp.zeros((t_pad, half), F32), sin, jnp.zeros((t_pad, pad), F32)], axis=1)
    reps = LANES // period
    return tuple(jnp.tile(a, (1, reps)) for a in (c, sa, sb))


def _a_weights(w_in):
    d = w_in.shape[0]
    lo = A_Q + 2 * A_KV
    zeros = jnp.zeros((d, IDX_DIM), w_in.dtype)
    cols = [w_in[:, :lo]]
    for hh in range(IDX_HEADS):
        qi = w_in[:, lo + hh * IDX_DIM:lo + (hh + 1) * IDX_DIM]
        cols += [qi, qi, qi, zeros]
    lo += IDX_HEADS * IDX_DIM
    ki = w_in[:, lo:lo + IDX_DIM]
    cols += [ki, ki, ki, zeros]
    lo += IDX_DIM
    cols += [w_in[:, lo:lo + IDX_HEADS], jnp.zeros((d, LANES - IDX_HEADS), w_in.dtype)]
    return jnp.concatenate(cols, axis=1).astype(BF16)


def kernel(x, meta_tokens, a_norm, a_w_in, a_q_norm, a_k_norm, a_w_o, kv_norm, kv_w,
           b_norm, b_w_q, b_w_o, ffn_norm, peer_w_q, peer_subkeys, peer_u, peer_v):
    b, s, d = x.shape
    t = s + N_META
    t_pad = -(-t // Q_BLOCK) * Q_BLOCK
    n = b * t_pad
    depth = ffn_norm.shape[0]
    n_a = a_norm.shape[0]
    topk = min(INDEX_TOPK, s // 4)

    h = jnp.concatenate([jnp.broadcast_to(meta_tokens[None].astype(x.dtype), (b, N_META, d)), x], axis=1)
    h = jnp.pad(h, ((0, 0), (0, t_pad - t), (0, 0))).reshape(n, d)

    tabs = _rope_tables(t_pad, HEAD_DIM, LANES) + _rope_tables(t_pad, IDX_DIM, IDX_DIM)
    tm_a = t_pad // 11 if t_pad % 11 == 0 and (t_pad // 11) % 8 == 0 else Q_BLOCK
    tm = 512 if n % 512 == 0 else Q_BLOCK
    tm_e = 768 if n % 768 == 0 else tm

    kv = None
    for layer in range(depth):
        if layer < n_a:
            q, k, v, ai, kb, wi = a_proj(h, a_norm[layer], _a_weights(a_w_in[layer]),
                                         a_q_norm[layer], a_k_norm[layer], tabs, b, tm_a)
            o = dsa_attention(q, k, v, ai, kb, wi, b, topk)
            h = out_proj(h, o, a_w_o[layer].astype(BF16), tm)
        else:
            j = layer - n_a
            q = rms_proj(h, b_norm[j], b_w_q[j].astype(BF16), tm, BF16)
            o = sb_attention(q, kv, b)
            h = out_proj(h, o, b_w_o[j].astype(BF16), tm)
        h = peer_ffn(h, ffn_norm[layer], peer_w_q[layer].astype(BF16),
                     peer_subkeys[layer].astype(BF16), peer_u[layer].astype(BF16),
                     peer_v[layer].astype(BF16), tm, tm_e, 4 * N_KEYS)
        if layer == n_a - 1:
            kv = rms_proj(h, kv_norm, kv_w.astype(BF16), tm, BF16)
    return h.reshape(b, t_pad, d)[:, N_META:N_META + s]
```

```python
import functools

import jax
import jax.numpy as jnp
from jax import lax
from jax.experimental import pallas as pl
from jax.experimental.pallas import tpu as pltpu

F32 = jnp.float32
BF16 = jnp.bfloat16
I32 = jnp.int32
U32 = jnp.uint32

D_MODEL = 1024
N_META = 16
N_HEADS = 8
HEAD_DIM = 128
N_KV_HEADS = 2
GROUP = N_HEADS // N_KV_HEADS
ROPE_THETA = 500000.0
IDX_HEADS = 4
IDX_DIM = 64
INDEX_TOPK = 256
Q_BLOCK = 128
PEER_HEADS = 8
PEER_HALF = 128
N_KEYS = 128
PEER_TOPK = 16
RMS_EPS = 1e-6

LANES = 128
KCH = 2 * LANES
INT_MIN = -(2 ** 31)
NEG_BIG = -1e30
EXP_UNDERFLOW = 104.0
VMEM_LIMIT = 48 * 1024 * 1024

A_Q = N_HEADS * HEAD_DIM
A_KV = N_KV_HEADS * HEAD_DIM
A_AI = IDX_HEADS * KCH
A_KB = KCH
GROWS = GROUP * Q_BLOCK
SCALE = HEAD_DIM ** -0.5
NT_DIMS = (((1,), (1,)), ((), ()))


def _params(sem):
    return pltpu.CompilerParams(dimension_semantics=sem, vmem_limit_bytes=VMEM_LIMIT)


def _rms(x, gain):
    return x * lax.rsqrt(jnp.mean(x * x, axis=-1, keepdims=True) + RMS_EPS) * gain


def _rms_proj_body(h_ref, g_ref, w_ref, o_ref):
    y = _rms(h_ref[...], g_ref[...]).astype(BF16)
    o_ref[...] = jnp.dot(y, w_ref[...], preferred_element_type=F32).astype(o_ref.dtype)


def rms_proj(h, gain, w, tm, out_dtype):
    n, d = h.shape
    nout = w.shape[1]
    return pl.pallas_call(
        _rms_proj_body,
        grid=(n // tm,),
        in_specs=[pl.BlockSpec((tm, d), lambda i: (i, 0)),
                  pl.BlockSpec((1, d), lambda i: (0, 0)),
                  pl.BlockSpec((d, nout), lambda i: (0, 0))],
        out_specs=pl.BlockSpec((tm, nout), lambda i: (i, 0)),
        out_shape=jax.ShapeDtypeStruct((n, nout), out_dtype),
        compiler_params=_params(("parallel",)),
        name="rms_proj",
    )(h, gain.reshape(1, d), w)


def _peer_q_body(h_ref, g_ref, w_ref, o_ref):
    y = _rms(h_ref[...], g_ref[...]).astype(BF16)
    for hc in range(2 * PEER_HEADS):
        o_ref[hc] = jnp.dot(y, w_ref[:, hc * LANES:(hc + 1) * LANES],
                            preferred_element_type=F32).astype(o_ref.dtype)


def peer_query(h, gain, w, tm):
    n, d = h.shape
    nhc = 2 * PEER_HEADS
    return pl.pallas_call(
        _peer_q_body,
        grid=(n // tm,),
        in_specs=[pl.BlockSpec((tm, d), lambda i: (i, 0)),
                  pl.BlockSpec((1, d), lambda i: (0, 0)),
                  pl.BlockSpec((d, nhc * LANES), lambda i: (0, 0))],
        out_specs=pl.BlockSpec((nhc, tm, LANES), lambda i: (0, i, 0)),
        out_shape=jax.ShapeDtypeStruct((nhc, n, LANES), BF16),
        compiler_params=_params(("parallel",)),
        name="peer_query",
    )(h, gain.reshape(1, d), w)


def _out_proj_body(h_ref, o_ref, w_ref, out_ref):
    out_ref[...] = h_ref[...] + jnp.dot(o_ref[...], w_ref[...], preferred_element_type=F32)


def out_proj(h, o, w, tm):
    n, d = h.shape
    k = o.shape[1]
    return pl.pallas_call(
        _out_proj_body,
        grid=(n // tm,),
        in_specs=[pl.BlockSpec((tm, d), lambda i: (i, 0)),
                  pl.BlockSpec((tm, k), lambda i: (i, 0)),
                  pl.BlockSpec((k, d), lambda i: (0, 0))],
        out_specs=pl.BlockSpec((tm, d), lambda i: (i, 0)),
        out_shape=jax.ShapeDtypeStruct((n, d), F32),
        compiler_params=_params(("parallel",)),
        name="out_proj",
    )(h, o, w)


def _rope(x, c, sa, sb, half):
    return x * c + pltpu.roll(x, LANES - half, 1) * sa + pltpu.roll(x, half, 1) * sb


def _a_proj_body(h_ref, g_ref, w_ref, qg_ref, kg_ref, hc_ref, hsa_ref, hsb_ref,
                 ic_ref, isa_ref, isb_ref, q_ref, k_ref, v_ref, ai_ref, kb_ref, wi_ref):
    y = _rms(h_ref[...], g_ref[...]).astype(BF16)
    hc, hsa, hsb = hc_ref[...], hsa_ref[...], hsb_ref[...]
    ic, isa, isb = ic_ref[...], isa_ref[...], isb_ref[...]
    for hh in range(N_HEADS + N_KV_HEADS):
        lo = hh * LANES
        p = jnp.dot(y, w_ref[:, lo:lo + LANES], preferred_element_type=F32)
        gain = qg_ref[...] if hh < N_HEADS else kg_ref[...]
        p = _rope(_rms(p, gain), hc, hsa, hsb, HEAD_DIM // 8)
        if hh < N_HEADS:
            q_ref[:, lo:lo + LANES] = p.astype(BF16)
        else:
            k_ref[:, lo - A_Q:lo - A_Q + LANES] = p.astype(BF16)
    lo = A_Q + A_KV
    v_ref[...] = jnp.dot(y, w_ref[:, lo:lo + A_KV], preferred_element_type=F32).astype(BF16)
    lo += A_KV
    first_group = lax.broadcasted_iota(I32, (y.shape[0], LANES), 1) < IDX_DIM
    n_q_slabs = A_AI // LANES
    for s in range(n_q_slabs + A_KB // LANES):
        p = jnp.dot(y, w_ref[:, lo + s * LANES:lo + (s + 1) * LANES], preferred_element_type=F32)
        p = _rope(p, ic, isa, isb, IDX_DIM // 8)
        if s < n_q_slabs:
            p = p * (IDX_DIM ** -0.5)
        hi = p.astype(BF16).astype(F32)
        if s < n_q_slabs:
            val = jnp.where(first_group, hi, p - hi) if s % 2 == 0 else hi
            ai_ref[:, s * LANES:(s + 1) * LANES] = val.astype(BF16)
        else:
            val = hi if s == n_q_slabs else p - hi
            kb_ref[:, (s - n_q_slabs) * LANES:(s - n_q_slabs + 1) * LANES] = val.astype(BF16)
    lo += A_AI + A_KB
    wi_ref[...] = jnp.dot(y, w_ref[:, lo:lo + LANES], preferred_element_type=F32) * (IDX_HEADS ** -0.5)


def a_proj(h, gain, w_all, q_gain, k_gain, tabs, batch, tm):
    n, d = h.shape
    t_pad = n // batch
    nt = t_pad // tm
    row = lambda b, j: (b * nt + j, 0)
    tab = pl.BlockSpec((tm, LANES), lambda b, j: (j, 0))
    const = lambda shape: pl.BlockSpec(shape, lambda b, j: (0, 0))
    widths = (A_Q, A_KV, A_KV, A_AI, A_KB)
    return pl.pallas_call(
        _a_proj_body,
        grid=(batch, nt),
        in_specs=[pl.BlockSpec((tm, d), row), const((1, d)), const(w_all.shape),
                  const((1, LANES)), const((1, LANES)), tab, tab, tab, tab, tab, tab],
        out_specs=[pl.BlockSpec((tm, w), row) for w in widths] + [pl.BlockSpec((tm, LANES), row)],
        out_shape=[jax.ShapeDtypeStruct((n, w), BF16) for w in widths]
        + [jax.ShapeDtypeStruct((n, LANES), F32)],
        compiler_params=_params(("parallel", "parallel")),
        name="a_proj",
    )(h, gain.reshape(1, d), w_all, q_gain.reshape(1, LANES), k_gain.reshape(1, LANES), *tabs)


def _dsa_body(q_ref, ai_ref, wi_ref, k_ref, ve_ref, kb_ref, o_ref,
              key_scr, bias_scr, wib_scr, qs_scr, mx_scr, acc_scr, *, topk):
    i = pl.program_id(1)
    nch = i + 1
    nch2 = (i + 2) // 2
    row2 = lax.broadcasted_iota(I32, (Q_BLOCK, KCH), 0)
    col2 = lax.broadcasted_iota(I32, (Q_BLOCK, KCH), 1)

    for hh in range(IDX_HEADS):
        wib_scr[hh] = jnp.broadcast_to(wi_ref[:, hh:hh + 1], (Q_BLOCK, KCH))
    for hh in range(N_HEADS):
        g, j = divmod(hh, GROUP)
        qs_scr[g, j * Q_BLOCK:(j + 1) * Q_BLOCK, :] = q_ref[:, hh * LANES:(hh + 1) * LANES]

    def score_chunk(c, carry):
        base = pl.multiple_of(c * KCH, KCH)
        kc = kb_ref[pl.ds(base, KCH), :]
        s = jnp.zeros((Q_BLOCK, KCH), F32)
        for hh in range(IDX_HEADS):
            d = lax.dot_general(ai_ref[:, hh * KCH:(hh + 1) * KCH], kc, NT_DIMS,
                                preferred_element_type=F32)
            s = s + jnp.maximum(d, 0.0) * wib_scr[hh]
        s = jnp.where(s == 0.0, 0.0, s)
        bits = pltpu.bitcast(s, I32)
        key = jnp.where(bits < 0, bits ^ jnp.int32(0x7FFFFFFF), bits)
        causal = (c * KCH + col2) <= (i * Q_BLOCK + row2)
        key = jnp.where(causal, key, jnp.int32(INT_MIN))
        key_scr[2 * c] = key[:, :LANES]
        key_scr[2 * c + 1] = key[:, LANES:]
        return carry

    lax.fori_loop(0, nch2, score_chunk, 0)

    def count(pred_fn):
        def body(c, cnt):
            return cnt + jnp.where(pred_fn(key_scr[c]), 1.0, 0.0)
        cnt = lax.fori_loop(0, nch, body, jnp.zeros((Q_BLOCK, LANES), F32))
        return jnp.sum(cnt, axis=1, keepdims=True)

    def bs_pass(b, ans):
        cand = ans + lax.shift_left(jnp.int32(1), jnp.int32(31) - b)
        candb = jnp.broadcast_to(cand, (Q_BLOCK, LANES))
        tot = count(lambda key: key >= candb)
        return jnp.where(tot >= float(topk), cand, ans)

    ans = lax.fori_loop(0, 32, bs_pass, jnp.full((Q_BLOCK, 1), INT_MIN, I32))
    ansb = jnp.broadcast_to(ans, (Q_BLOCK, LANES))
    n_gt = count(lambda key: key > ansb)
    need = jnp.where(ans == jnp.int32(INT_MIN), 0.0, float(topk) - n_gt)
    needb = jnp.broadcast_to(need, (Q_BLOCK, LANES))

    r2 = lax.broadcasted_iota(I32, (LANES, 2 * LANES), 0)
    c2 = lax.broadcasted_iota(I32, (LANES, 2 * LANES), 1)
    pre_tot = jnp.where((r2 < c2) | (c2 >= LANES), 1.0, 0.0).astype(BF16)

    def bias_chunk(c, ties_seen):
        key = key_scr[c]
        eq = key == ansb
        pt = jnp.dot(jnp.where(eq, 1.0, 0.0).astype(BF16), pre_tot, preferred_element_type=F32)
        tie_ok = (ties_seen + pt[:, :LANES]) < needb
        bias_scr[c] = jnp.where(key > ansb, 0.0,
                                jnp.where(eq, jnp.where(tie_ok, 0.0, NEG_BIG), NEG_BIG))
        return ties_seen + pt[:, LANES:]

    lax.fori_loop(0, nch, bias_chunk, jnp.zeros((Q_BLOCK, LANES), F32))
    bias_scr[nch] = jnp.full((Q_BLOCK, LANES), NEG_BIG, F32)

    def logits(c, g, bias4):
        base = pl.multiple_of(c * KCH, KCH)
        s = lax.dot_general(qs_scr[g], k_ref[pl.ds(base, KCH), g * LANES:(g + 1) * LANES],
                            NT_DIMS, preferred_element_type=F32)
        return s * SCALE + bias4

    def bias_rows(c):
        b2 = jnp.concatenate([bias_scr[2 * c], bias_scr[2 * c + 1]], axis=1)
        return jnp.tile(b2, (GROUP, 1))

    mx_scr[...] = jnp.full(mx_scr.shape, NEG_BIG, F32)

    def max_chunk(c, carry):
        bias4 = bias_rows(c)
        for g in range(N_KV_HEADS):
            s = logits(c, g, bias4)
            mx_scr[g] = jnp.maximum(mx_scr[g], jnp.maximum(s[:, :LANES], s[:, LANES:]))
        return carry

    lax.fori_loop(0, nch2, max_chunk, 0)
    for g in range(N_KV_HEADS):
        mx_scr[g] = jnp.broadcast_to(jnp.max(mx_scr[g], axis=1, keepdims=True), (GROWS, LANES))

    acc_scr[...] = jnp.zeros_like(acc_scr)

    def acc_chunk(c, carry):
        bias4 = bias_rows(c)
        base = pl.multiple_of(c * KCH, KCH)
        for g in range(N_KV_HEADS):
            m = mx_scr[g]
            p = jnp.exp(logits(c, g, bias4) - jnp.concatenate([m, m], axis=1))
            acc_scr[g] += jnp.dot(p.astype(BF16), ve_ref[pl.ds(base, KCH), g * KCH:(g + 1) * KCH],
                                  preferred_element_type=F32)
        return carry

    lax.fori_loop(0, nch2, acc_chunk, 0)

    for hh in range(N_HEADS):
        g, j = divmod(hh, GROUP)
        a = acc_scr[g, j * Q_BLOCK:(j + 1) * Q_BLOCK, :]
        o_ref[:, hh * LANES:(hh + 1) * LANES] = (a[:, :LANES] / a[:, LANES:]).astype(BF16)


def _pad_keys(a, batch, tk):
    t_pad = a.shape[0] // batch
    a = a.reshape(batch, t_pad, a.shape[1])
    return jnp.pad(a, ((0, 0), (0, tk - t_pad), (0, 0))).reshape(batch * tk, a.shape[2])


def dsa_attention(q, k, v, ai, kb, wi, batch, topk):
    n = q.shape[0]
    t_pad = n // batch
    nq = t_pad // Q_BLOCK
    tk = ((nq + 1) // 2) * KCH
    ones = jnp.ones((n, LANES), v.dtype)
    ve = jnp.concatenate([v[:, :LANES], ones, v[:, LANES:], ones], axis=1)
    k, ve, kb = _pad_keys(k, batch, tk), _pad_keys(ve, batch, tk), _pad_keys(kb, batch, tk)
    blk = lambda b, i: (b * nq + i, 0)
    full = lambda b, i: (b, 0)
    return pl.pallas_call(
        functools.partial(_dsa_body, topk=topk),
        grid=(batch, nq),
        in_specs=[pl.BlockSpec((Q_BLOCK, A_Q), blk), pl.BlockSpec((Q_BLOCK, A_AI), blk),
                  pl.BlockSpec((Q_BLOCK, LANES), blk),
                  pl.BlockSpec((tk, A_KV), full), pl.BlockSpec((tk, 2 * KCH), full),
                  pl.BlockSpec((tk, A_KB), full)],
        out_specs=pl.BlockSpec((Q_BLOCK, A_Q), blk),
        out_shape=jax.ShapeDtypeStruct((n, A_Q), BF16),
        scratch_shapes=[pltpu.VMEM((nq + 2, Q_BLOCK, LANES), I32),
                        pltpu.VMEM((nq + 2, Q_BLOCK, LANES), F32),
                        pltpu.VMEM((IDX_HEADS, Q_BLOCK, KCH), F32),
                        pltpu.VMEM((N_KV_HEADS, GROWS, LANES), BF16),
                        pltpu.VMEM((N_KV_HEADS, GROWS, LANES), F32),
                        pltpu.VMEM((N_KV_HEADS, GROWS, KCH), F32)],
        compiler_params=_params(("parallel", "arbitrary")),
        name="dsa_attention",
    )(q, ai, wi, k, ve, kb)


def _sb_body(q_ref, k_ref, v_ref, o_ref, qs_scr, run_scr, acc_scr):
    i = pl.program_id(1)
    for hh in range(N_HEADS):
        g, j = divmod(hh, GROUP)
        qs_scr[g, j * Q_BLOCK:(j + 1) * Q_BLOCK, :] = q_ref[:, hh * LANES:(hh + 1) * LANES]
    run_scr[...] = jnp.zeros_like(run_scr)
    acc_scr[...] = jnp.zeros_like(acc_scr)

    r2 = lax.broadcasted_iota(I32, (2 * LANES, 2 * LANES), 0) % LANES
    c2 = lax.broadcasted_iota(I32, (2 * LANES, 2 * LANES), 1)
    suf_tot = jnp.where((r2 > c2) | (c2 >= LANES), 1.0, 0.0).astype(BF16)

    def step(c, causal):
        base = pl.multiple_of(c * LANES, LANES)
        for g in range(N_KV_HEADS):
            z = lax.dot_general(qs_scr[g], k_ref[pl.ds(base, LANES), g * LANES:(g + 1) * LANES],
                                NT_DIMS, preferred_element_type=F32) * SCALE
            sp = jnp.maximum(z, 0.0) + jnp.log1p(jnp.exp(-jnp.abs(z)))
            lk = -sp if causal is None else jnp.where(causal, -sp, 0.0)
            hi = lk.astype(BF16)
            lo = (lk - hi.astype(F32)).astype(BF16)
            st = jnp.dot(jnp.concatenate([hi, lo], axis=1), suf_tot, preferred_element_type=F32)
            a = jnp.exp(z - sp + st[:, :LANES] + run_scr[g])
            if causal is not None:
                a = jnp.where(causal, a, 0.0)
            acc_scr[g] += jnp.dot(a.astype(BF16), v_ref[pl.ds(base, LANES), g * LANES:(g + 1) * LANES],
                                  preferred_element_type=F32)
            run_scr[g] += st[:, LANES:]

    row = lax.broadcasted_iota(I32, (GROWS, LANES), 0) % Q_BLOCK
    col = lax.broadcasted_iota(I32, (GROWS, LANES), 1)
    step(i, col < row)

    def cond(state):
        t, live = state
        return jnp.logical_and(t <= i, live)

    def body(state):
        t, _ = state
        step(i - t, None)
        worst = jnp.max(jnp.maximum(run_scr[0], run_scr[1]))
        return t + 1, worst > -EXP_UNDERFLOW

    lax.while_loop(cond, body, (jnp.int32(1), jnp.bool_(True)))

    for hh in range(N_HEADS):
        g, j = divmod(hh, GROUP)
        o_ref[:, hh * LANES:(hh + 1) * LANES] = acc_scr[g, j * Q_BLOCK:(j + 1) * Q_BLOCK, :].astype(BF16)


def sb_attention(q, kv, batch):
    n = q.shape[0]
    t_pad = n // batch
    nq = t_pad // Q_BLOCK
    blk = lambda b, i: (b * nq + i, 0)
    return pl.pallas_call(
        _sb_body,
        grid=(batch, nq),
        in_specs=[pl.BlockSpec((Q_BLOCK, A_Q), blk),
                  pl.BlockSpec((t_pad, A_KV), lambda b, i: (b, 0)),
                  pl.BlockSpec((t_pad, A_KV), lambda b, i: (b, 1))],
        out_specs=pl.BlockSpec((Q_BLOCK, A_Q), blk),
        out_shape=jax.ShapeDtypeStruct((n, A_Q), BF16),
        scratch_shapes=[pltpu.VMEM((N_KV_HEADS, GROWS, LANES), BF16),
                        pltpu.VMEM((N_KV_HEADS, GROWS, LANES), F32),
                        pltpu.VMEM((N_KV_HEADS, GROWS, LANES), F32)],
        compiler_params=_params(("parallel", "arbitrary")),
        name="sb_attention",
    )(q, kv, kv)


def _extract_top(s, kidx, pidx):
    cur = s
    rank = jnp.full(s.shape, float(PEER_TOPK), F32)
    tops = jnp.zeros((PEER_TOPK, s.shape[1]), F32)
    for r in range(PEER_TOPK):
        m = jnp.max(cur, axis=0, keepdims=True)
        first = jnp.min(jnp.where(cur == m, kidx, float(N_KEYS)), axis=0, keepdims=True)
        hit = kidx == first
        rank = jnp.where(hit, float(r), rank)
        cur = jnp.where(hit, -jnp.inf, cur)
        tops = jnp.where(pidx == float(r), m, tops)
    return rank, tops


SUBLANES = 8


def _bitonic_merge(v, start):
    j = start
    while j >= 1:
        for i in range(len(v)):
            if i ^ j > i:
                v[i], v[i ^ j] = jnp.maximum(v[i], v[i ^ j]), jnp.minimum(v[i], v[i ^ j])
        j //= 2


def _top_values(s):
    n = N_KEYS // SUBLANES
    assert n == PEER_TOPK
    v = [s[k * SUBLANES:(k + 1) * SUBLANES, :] for k in range(n)]
    k = 2
    while k <= n:
        j = k // 2
        while j >= 1:
            for i in range(n):
                if i ^ j > i:
                    hi, lo = jnp.maximum(v[i], v[i ^ j]), jnp.minimum(v[i], v[i ^ j])
                    v[i], v[i ^ j] = (hi, lo) if (i & k) == 0 else (lo, hi)
            j //= 2
        k *= 2
    shift = SUBLANES // 2
    while shift >= 1:
        v = [jnp.maximum(v[k], pltpu.roll(v[n - 1 - k], shift, 0)) for k in range(n)]
        _bitonic_merge(v, n // 2)
        shift //= 2
    return v


def _rank_by_value(s, tops, pidx):
    n = len(tops)
    groups, at_least = [], jnp.zeros((SUBLANES, s.shape[1]), F32)
    for g in range(N_KEYS // SUBLANES):
        sg = s[g * SUBLANES:(g + 1) * SUBLANES, :]
        r = jnp.zeros_like(sg)
        for k in range(n):
            r = r + jnp.where(tops[k] > sg, 1.0, 0.0)
        groups.append(r)
        at_least = at_least + jnp.where(sg >= tops[n - 1], 1.0, 0.0)
    tied = jnp.where(jnp.sum(at_least, axis=0, keepdims=True) == float(n), 0.0, 1.0)
    vals = jnp.zeros((n, s.shape[1]), F32)
    for k in range(n):
        vals = jnp.where(pidx == float(k), jnp.tile(tops[k], (n // SUBLANES, 1)), vals)
        if k + 1 < n:
            tied = jnp.maximum(tied, jnp.max(jnp.where(tops[k] > tops[k + 1], 0.0, 1.0), axis=0, keepdims=True))
    return jnp.concatenate(groups, axis=0), vals, tied


def _bf16_bits(x):
    return pltpu.bitcast(x.astype(BF16).astype(F32), U32)


def _route_body(q_ref, sk_ref, el_ref, e2_ref, r2_ref, score_scr, rank_scr, vals_scr, *, group):
    kidx = lax.broadcasted_iota(I32, (N_KEYS, LANES), 0).astype(F32)
    pidx = lax.broadcasted_iota(I32, (PEER_TOPK, LANES), 0).astype(F32)

    def select(hh, slot):
        tied = jnp.zeros((1, LANES), F32)
        for c in range(2):
            s = lax.dot_general(sk_ref[c], q_ref[2 * hh + c], NT_DIMS, preferred_element_type=F32)
            score_scr[slot, c] = s
            rank_scr[slot, c], vals_scr[slot, c], t = _rank_by_value(s, _top_values(s), pidx)
            tied = jnp.maximum(tied, t)
        return tied

    def finish(hh, slot):
        s1, s2 = score_scr[slot, 0], score_scr[slot, 1]
        rank1, a = rank_scr[slot, 0], vals_scr[slot, 0]
        rank2, b = rank_scr[slot, 1], vals_scr[slot, 1]

        def take(_, st):
            qmax, head_val = st
            m = jnp.max(head_val, axis=0, keepdims=True)
            first = jnp.min(jnp.where(head_val == m, pidx, float(PEER_TOPK)), axis=0, keepdims=True)
            hit = pidx == first
            qmax = jnp.where(hit, qmax + 1.0, qmax)
            qsel = jnp.sum(jnp.where(hit, qmax, 0.0), axis=0, keepdims=True)
            bnext = jnp.sum(jnp.where(pidx == qsel, b, 0.0), axis=0, keepdims=True)
            bnext = jnp.where(qsel >= float(PEER_TOPK), -jnp.inf, bnext)
            return qmax, jnp.where(hit, a + bnext, head_val)

        qmax, _ = lax.fori_loop(0, PEER_TOPK, take,
                                (jnp.zeros((PEER_TOPK, LANES), F32), a + b[0:1, :]),
                                unroll=True)

        ea = jnp.exp(a - a[0:1, :])
        eb = jnp.exp(b - b[0:1, :])
        pref = jnp.zeros((PEER_TOPK, LANES), F32)
        run = jnp.zeros((1, LANES), F32)
        for q in range(PEER_TOPK):
            run = run + eb[q:q + 1, :]
            pref = jnp.where(qmax == float(q + 1), run, pref)
        z = jnp.sum(ea * pref, axis=0, keepdims=True)

        lim = jnp.zeros((N_KEYS, LANES), F32)
        for p in range(PEER_TOPK):
            lim = jnp.where(rank1 == float(p), qmax[p:p + 1, :], lim)
        e1 = jnp.where(rank1 < float(PEER_TOPK), jnp.exp(s1 - a[0:1, :]), 0.0) / z
        el_ref[hh] = _bf16_bits(e1) | (_bf16_bits(lim) >> 16)
        e2_ref[hh] = jnp.where(rank2 < float(PEER_TOPK), jnp.exp(s2 - b[0:1, :]), 0.0)
        r2_ref[hh] = rank2

    def head_group(hg, carry):
        tied = jnp.zeros((1, LANES), F32)
        for slot in range(group):
            tied = jnp.maximum(tied, select(hg * group + slot, slot))

        @pl.when(jnp.max(tied) > 0.0)
        def _():
            for slot in range(group):
                for c in range(2):
                    rank_scr[slot, c], vals_scr[slot, c] = _extract_top(score_scr[slot, c], kidx, pidx)

        for slot in range(group):
            finish(hg * group + slot, slot)
        return carry

    lax.fori_loop(0, PEER_HEADS // group, head_group, 0)


def peer_route(q, subkeys):
    nhc, n, _ = q.shape
    tab = pl.BlockSpec((PEER_HEADS, N_KEYS, LANES), lambda t: (0, 0, t))
    words = jax.ShapeDtypeStruct((PEER_HEADS, N_KEYS, n), U32)
    halves = jax.ShapeDtypeStruct((PEER_HEADS, N_KEYS, n), F32)
    group = 4
    return pl.pallas_call(
        functools.partial(_route_body, group=group),
        grid=(n // LANES,),
        in_specs=[pl.BlockSpec((nhc, LANES, LANES), lambda t: (0, t, 0)),
                  pl.BlockSpec((2, N_KEYS, PEER_HALF), lambda t: (0, 0, 0))],
        out_specs=[tab, tab, tab],
        out_shape=[words, halves, halves],
        scratch_shapes=[pltpu.VMEM((group, 2, N_KEYS, LANES), F32),
                        pltpu.VMEM((group, 2, N_KEYS, LANES), F32),
                        pltpu.VMEM((group, 2, PEER_TOPK, LANES), F32)],
        compiler_params=_params(("parallel",)),
        name="peer_route",
    )(q, subkeys)


def _gelu(x):
    return 0.5 * x * (1.0 + lax.erf(x * (2.0 ** -0.5)))


def _expert_body(h_ref, g_ref, u_ref, v_ref, el_ref, e2_ref, r2_ref, out_ref,
                 xt_scr, acc_scr, ht_scr, pt_scr, *, tm, eb, ne):
    e = pl.program_id(1)
    nlc = tm // LANES
    ndc = acc_scr.shape[0]

    nii = eb // N_KEYS

    def mm1(slab):
        res = jnp.dot(u_ref[...], xt_scr[...], preferred_element_type=F32)
        for tc in range(nlc):
            ht_scr[slab % 2, tc] = res[:, tc * LANES:(tc + 1) * LANES]

    def mm2(slab):
        lhs = jnp.concatenate([pt_scr[slab % 2, tc] for tc in range(nlc)], axis=1)
        res = lax.dot_general(lhs, v_ref[...], (((0,), (0,)), ((), ())), preferred_element_type=F32)
        for dc in range(ndc):
            acc_scr[dc] += res[:, dc * LANES:(dc + 1) * LANES]

    def gate(slab):
        par = slab % 2
        for ii in range(nii):
            el_i = el_ref[slab * nii + ii]
            e1_i = pltpu.bitcast(el_i & jnp.uint32(0xFFFF0000), F32)
            lim_i = pltpu.bitcast(el_i << 16, F32)
            for tc in range(nlc):
                sl = slice(tc * LANES, (tc + 1) * LANES)
                g = jnp.zeros((N_KEYS, LANES), F32)
                for hh in range(PEER_HEADS):
                    picked = jnp.where(r2_ref[hh, :, sl] < lim_i[hh:hh + 1, sl], e2_ref[hh, :, sl], 0.0)
                    g = g + picked * e1_i[hh:hh + 1, sl]
                act = _gelu(ht_scr[par, tc, ii * N_KEYS:(ii + 1) * N_KEYS, :])
                pt_scr[par, tc, ii * N_KEYS:(ii + 1) * N_KEYS, :] = (act * g).astype(BF16)

    @pl.when(e == 0)
    def _():
        y = _rms(h_ref[...], g_ref[...])
        xt_scr[...] = y.T.astype(BF16)
        acc_scr[...] = jnp.zeros_like(acc_scr)
        mm1(e)

    @pl.when(e == 1)
    def _():
        mm1(e)
        gate(e - 1)

    @pl.when(jnp.logical_and(e >= 2, e < ne))
    def _():
        mm1(e)
        mm2(e - 2)
        gate(e - 1)

    @pl.when(e == ne)
    def _():
        mm2(e - 2)
        gate(e - 1)

    @pl.when(e == ne + 1)
    def _():
        mm2(e - 2)
        out_ref[...] = h_ref[...] + jnp.concatenate([acc_scr[dc] for dc in range(ndc)], axis=1)


def peer_experts(h, gain, u, v, tabs, tm, eb):
    n, d = h.shape
    ne = u.shape[0] // eb
    nlc = tm // LANES
    tab = pl.BlockSpec((PEER_HEADS, N_KEYS, tm), lambda t, e: (0, 0, t))
    rtab = pl.BlockSpec((N_KEYS, PEER_HEADS, tm), lambda t, e: (0, 0, t))
    el, e2, r2 = tabs
    el = jnp.transpose(el, (1, 0, 2))
    return pl.pallas_call(
        functools.partial(_expert_body, tm=tm, eb=eb, ne=ne),
        grid=(n // tm, ne + 2),
        in_specs=[pl.BlockSpec((tm, d), lambda t, e: (t, 0)),
                  pl.BlockSpec((1, d), lambda t, e: (0, 0)),
                  pl.BlockSpec((eb, d), lambda t, e: (jnp.minimum(e, ne - 1), 0)),
                  pl.BlockSpec((eb, d), lambda t, e: (jnp.clip(e - 2, 0, ne - 1), 0)),
                  rtab, tab, tab],
        out_specs=pl.BlockSpec((tm, d), lambda t, e: (t, 0)),
        out_shape=jax.ShapeDtypeStruct((n, d), F32),
        scratch_shapes=[pltpu.VMEM((d, tm), BF16), pltpu.VMEM((d // LANES, tm, LANES), F32),
                        pltpu.VMEM((2, nlc, eb, LANES), F32), pltpu.VMEM((2, nlc, eb, LANES), BF16)],
        compiler_params=_params(("parallel", "arbitrary")),
        name="peer_experts",
    )(h, gain.reshape(1, d), u, v, el, e2, r2)


def peer_ffn(h, gain, w_q, subkeys, u, v, tm_q, tm_e, eb):
    q = peer_query(h, gain, w_q, tm_q)
    tabs = peer_route(q, subkeys)
    return peer_experts(h, gain, u, v, tabs, tm_e, eb)


def _rope_tables(t_pad, dim, period):
    r = dim // 4
    half = r // 2
    pos = jnp.arange(t_pad, dtype=F32)
    inv = ROPE_THETA ** (-jnp.arange(half, dtype=F32) * 2.0 / r)
    ang = pos[:, None] * inv[None, :]
    cos, sin = jnp.cos(ang), jnp.sin(ang)
    pad = period - 2 * half
    c = jnp.concatenate([cos, cos, jnp.ones((t_pad, pad), F32)], axis=1)
    sa = jnp.concatenate([-sin, jnp.zeros((t_pad, half + pad), F32)], axis=1)
    sb = jnp.concatenate([jnp.zeros((t_pad, half), F32), sin, jnp.zeros((t_pad, pad), F32)], axis=1)
    reps = LANES // period
    return tuple(jnp.tile(a, (1, reps)) for a in (c, sa, sb))


def _a_weights(w_in):
    d = w_in.shape[0]
    lo = A_Q + 2 * A_KV
    zeros = jnp.zeros((d, IDX_DIM), w_in.dtype)
    cols = [w_in[:, :lo]]
    for hh in range(IDX_HEADS):
        qi = w_in[:, lo + hh * IDX_DIM:lo + (hh + 1) * IDX_DIM]
        cols += [qi, qi, qi, zeros]
    lo += IDX_HEADS * IDX_DIM
    ki = w_in[:, lo:lo + IDX_DIM]
    cols += [ki, ki, ki, zeros]
    lo += IDX_DIM
    cols += [w_in[:, lo:lo + IDX_HEADS], jnp.zeros((d, LANES - IDX_HEADS), w_in.dtype)]
    return jnp.concatenate(cols, axis=1).astype(BF16)


def kernel(x, meta_tokens, a_norm, a_w_in, a_q_norm, a_k_norm, a_w_o, kv_norm, kv_w,
           b_norm, b_w_q, b_w_o, ffn_norm, peer_w_q, peer_subkeys, peer_u, peer_v):
    b, s, d = x.shape
    t = s + N_META
    t_pad = -(-t // Q_BLOCK) * Q_BLOCK
    n = b * t_pad
    depth = ffn_norm.shape[0]
    n_a = a_norm.shape[0]
    topk = min(INDEX_TOPK, s // 4)

    h = jnp.concatenate([jnp.broadcast_to(meta_tokens[None].astype(x.dtype), (b, N_META, d)), x], axis=1)
    h = jnp.pad(h, ((0, 0), (0, t_pad - t), (0, 0))).reshape(n, d)

    tabs = _rope_tables(t_pad, HEAD_DIM, LANES) + _rope_tables(t_pad, IDX_DIM, IDX_DIM)
    tm_a = t_pad // 11 if t_pad % 11 == 0 and (t_pad // 11) % 8 == 0 else Q_BLOCK
    tm = 512 if n % 512 == 0 else Q_BLOCK
    tm_e = 768 if n % 768 == 0 else tm

    kv = None
    for layer in range(depth):
        if layer < n_a:
            q, k, v, ai, kb, wi = a_proj(h, a_norm[layer], _a_weights(a_w_in[layer]),
                                         a_q_norm[layer], a_k_norm[layer], tabs, b, tm_a)
            o = dsa_attention(q, k, v, ai, kb, wi, b, topk)
            h = out_proj(h, o, a_w_o[layer].astype(BF16), tm)
        else:
            j = layer - n_a
            q = rms_proj(h, b_norm[j], b_w_q[j].astype(BF16), tm, BF16)
            o = sb_attention(q, kv, b)
            h = out_proj(h, o, b_w_o[j].astype(BF16), tm)
        h = peer_ffn(h, ffn_norm[layer], peer_w_q[layer].astype(BF16),
                     peer_subkeys[layer].astype(BF16), peer_u[layer].astype(BF16),
                     peer_v[layer].astype(BF16), tm, tm_e, 4 * N_KEYS)
        if layer == n_a - 1:
            kv = rms_proj(h, kv_norm, kv_w.astype(BF16), tm, BF16)
    return h.reshape(b, t_pad, d)[:, N_META:N_META + s]
```

```python
import functools

import jax
import jax.numpy as jnp
from jax import lax
from jax.experimental import pallas as pl
from jax.experimental.pallas import tpu as pltpu

F32 = jnp.float32
BF16 = jnp.bfloat16
I32 = jnp.int32
U32 = jnp.uint32

D_MODEL = 1024
N_META = 16
N_HEADS = 8
HEAD_DIM = 128
N_KV_HEADS = 2
GROUP = N_HEADS // N_KV_HEADS
ROPE_THETA = 500000.0
IDX_HEADS = 4
IDX_DIM = 64
INDEX_TOPK = 256
Q_BLOCK = 128
PEER_HEADS = 8
PEER_HALF = 128
N_KEYS = 128
PEER_TOPK = 16
RMS_EPS = 1e-6

LANES = 128
KCH = 2 * LANES
INT_MIN = -(2 ** 31)
NEG_BIG = -1e30
EXP_UNDERFLOW = 104.0
VMEM_LIMIT = 48 * 1024 * 1024

A_Q = N_HEADS * HEAD_DIM
A_KV = N_KV_HEADS * HEAD_DIM
A_AI = IDX_HEADS * KCH
A_KB = KCH
GROWS = GROUP * Q_BLOCK
SCALE = HEAD_DIM ** -0.5
NT_DIMS = (((1,), (1,)), ((), ()))


def _params(sem):
    return pltpu.CompilerParams(dimension_semantics=sem, vmem_limit_bytes=VMEM_LIMIT)


def _rms(x, gain):
    return x * lax.rsqrt(jnp.mean(x * x, axis=-1, keepdims=True) + RMS_EPS) * gain


def _rms_proj_body(h_ref, g_ref, w_ref, o_ref):
    y = _rms(h_ref[...], g_ref[...]).astype(BF16)
    o_ref[...] = jnp.dot(y, w_ref[...], preferred_element_type=F32).astype(o_ref.dtype)


def rms_proj(h, gain, w, tm, out_dtype):
    n, d = h.shape
    nout = w.shape[1]
    return pl.pallas_call(
        _rms_proj_body,
        grid=(n // tm,),
        in_specs=[pl.BlockSpec((tm, d), lambda i: (i, 0)),
                  pl.BlockSpec((1, d), lambda i: (0, 0)),
                  pl.BlockSpec((d, nout), lambda i: (0, 0))],
        out_specs=pl.BlockSpec((tm, nout), lambda i: (i, 0)),
        out_shape=jax.ShapeDtypeStruct((n, nout), out_dtype),
        compiler_params=_params(("parallel",)),
        name="rms_proj",
    )(h, gain.reshape(1, d), w)


def _peer_q_body(h_ref, g_ref, w_ref, o_ref):
    y = _rms(h_ref[...], g_ref[...]).astype(BF16)
    for hc in range(PEER_HEADS):
        res = jnp.dot(y, w_ref[:, hc * KCH:(hc + 1) * KCH], preferred_element_type=F32)
        o_ref[2 * hc] = res[:, :LANES].astype(o_ref.dtype)
        o_ref[2 * hc + 1] = res[:, LANES:].astype(o_ref.dtype)


def peer_query(h, gain, w, tm):
    n, d = h.shape
    nhc = 2 * PEER_HEADS
    return pl.pallas_call(
        _peer_q_body,
        grid=(n // tm,),
        in_specs=[pl.BlockSpec((tm, d), lambda i: (i, 0)),
                  pl.BlockSpec((1, d), lambda i: (0, 0)),
                  pl.BlockSpec((d, nhc * LANES), lambda i: (0, 0))],
        out_specs=pl.BlockSpec((nhc, tm, LANES), lambda i: (0, i, 0)),
        out_shape=jax.ShapeDtypeStruct((nhc, n, LANES), BF16),
        compiler_params=_params(("parallel",)),
        name="peer_query",
    )(h, gain.reshape(1, d), w)


def _out_proj_body(h_ref, o_ref, w_ref, out_ref):
    out_ref[...] = h_ref[...] + jnp.dot(o_ref[...], w_ref[...], preferred_element_type=F32)


def out_proj(h, o, w, tm):
    n, d = h.shape
    k = o.shape[1]
    return pl.pallas_call(
        _out_proj_body,
        grid=(n // tm,),
        in_specs=[pl.BlockSpec((tm, d), lambda i: (i, 0)),
                  pl.BlockSpec((tm, k), lambda i: (i, 0)),
                  pl.BlockSpec((k, d), lambda i: (0, 0))],
        out_specs=pl.BlockSpec((tm, d), lambda i: (i, 0)),
        out_shape=jax.ShapeDtypeStruct((n, d), F32),
        compiler_params=_params(("parallel",)),
        name="out_proj",
    )(h, o, w)


def _rope(x, c, sa, sb, half):
    return x * c + pltpu.roll(x, LANES - half, 1) * sa + pltpu.roll(x, half, 1) * sb


def _a_proj_body(h_ref, g_ref, w_ref, qg_ref, kg_ref, hc_ref, hsa_ref, hsb_ref,
                 ic_ref, isa_ref, isb_ref, q_ref, k_ref, v_ref, ai_ref, kb_ref, wi_ref):
    y = _rms(h_ref[...], g_ref[...]).astype(BF16)
    hc, hsa, hsb = hc_ref[...], hsa_ref[...], hsb_ref[...]
    ic, isa, isb = ic_ref[...], isa_ref[...], isb_ref[...]
    for hh in range(N_HEADS + N_KV_HEADS):
        lo = hh * LANES
        p = jnp.dot(y, w_ref[:, lo:lo + LANES], preferred_element_type=F32)
        gain = qg_ref[...] if hh < N_HEADS else kg_ref[...]
        p = _rope(_rms(p, gain), hc, hsa, hsb, HEAD_DIM // 8)
        if hh < N_HEADS:
            q_ref[:, lo:lo + LANES] = p.astype(BF16)
        else:
            k_ref[:, lo - A_Q:lo - A_Q + LANES] = p.astype(BF16)
    lo = A_Q + A_KV
    v_ref[...] = jnp.dot(y, w_ref[:, lo:lo + A_KV], preferred_element_type=F32).astype(BF16)
    lo += A_KV
    first_group = lax.broadcasted_iota(I32, (y.shape[0], LANES), 1) < IDX_DIM
    n_q_slabs = A_AI // LANES
    for s in range(n_q_slabs + A_KB // LANES):
        p = jnp.dot(y, w_ref[:, lo + s * LANES:lo + (s + 1) * LANES], preferred_element_type=F32)
        p = _rope(p, ic, isa, isb, IDX_DIM // 8)
        if s < n_q_slabs:
            p = p * (IDX_DIM ** -0.5)
        hi = p.astype(BF16).astype(F32)
        if s < n_q_slabs:
            val = jnp.where(first_group, hi, p - hi) if s % 2 == 0 else hi
            ai_ref[:, s * LANES:(s + 1) * LANES] = val.astype(BF16)
        else:
            val = hi if s == n_q_slabs else p - hi
            kb_ref[:, (s - n_q_slabs) * LANES:(s - n_q_slabs + 1) * LANES] = val.astype(BF16)
    lo += A_AI + A_KB
    wi_ref[...] = jnp.dot(y, w_ref[:, lo:lo + LANES], preferred_element_type=F32) * (IDX_HEADS ** -0.5)


def a_proj(h, gain, w_all, q_gain, k_gain, tabs, batch, tm):
    n, d = h.shape
    t_pad = n // batch
    nt = t_pad // tm
    row = lambda b, j: (b * nt + j, 0)
    tab = pl.BlockSpec((tm, LANES), lambda b, j: (j, 0))
    const = lambda shape: pl.BlockSpec(shape, lambda b, j: (0, 0))
    widths = (A_Q, A_KV, A_KV, A_AI, A_KB)
    return pl.pallas_call(
        _a_proj_body,
        grid=(batch, nt),
        in_specs=[pl.BlockSpec((tm, d), row), const((1, d)), const(w_all.shape),
                  const((1, LANES)), const((1, LANES)), tab, tab, tab, tab, tab, tab],
        out_specs=[pl.BlockSpec((tm, w), row) for w in widths] + [pl.BlockSpec((tm, LANES), row)],
        out_shape=[jax.ShapeDtypeStruct((n, w), BF16) for w in widths]
        + [jax.ShapeDtypeStruct((n, LANES), F32)],
        compiler_params=_params(("parallel", "parallel")),
        name="a_proj",
    )(h, gain.reshape(1, d), w_all, q_gain.reshape(1, LANES), k_gain.reshape(1, LANES), *tabs)


def _dsa_body(q_ref, ai_ref, wi_ref, k_ref, ve_ref, kb_ref, o_ref,
              key_scr, bias_scr, wib_scr, qs_scr, mx_scr, acc_scr, ties_scr, *, topk, qb):
    i = pl.program_id(1)
    qg = qb // Q_BLOCK
    nch = (i + 1) * qg
    nch2 = (nch + 1) // 2
    row2 = lax.broadcasted_iota(I32, (qb, KCH), 0)
    col2 = lax.broadcasted_iota(I32, (qb, KCH), 1)

    for hh in range(IDX_HEADS):
        wib_scr[hh] = jnp.broadcast_to(wi_ref[:, hh:hh + 1], (qb, KCH))
    for hh in range(N_HEADS):
        g, j = divmod(hh, GROUP)
        qs_scr[g, j * qb:(j + 1) * qb, :] = q_ref[:, hh * LANES:(hh + 1) * LANES]

    def score_chunk(c, carry):
        base = pl.multiple_of(c * KCH, KCH)
        kc = kb_ref[pl.ds(base, KCH), :]
        s = jnp.zeros((qb, KCH), F32)
        for hh in range(IDX_HEADS):
            d = lax.dot_general(ai_ref[:, hh * KCH:(hh + 1) * KCH], kc, NT_DIMS,
                                preferred_element_type=F32)
            s = s + jnp.maximum(d, 0.0) * wib_scr[hh]
        s = jnp.where(s == 0.0, 0.0, s)
        bits = pltpu.bitcast(s, I32)
        key = jnp.where(bits < 0, bits ^ jnp.int32(0x7FFFFFFF), bits)
        causal = (c * KCH + col2) <= (i * qb + row2)
        key = jnp.where(causal, key, jnp.int32(INT_MIN))
        key_scr[2 * c] = key[:, :LANES]
        key_scr[2 * c + 1] = key[:, LANES:]
        return carry

    lax.fori_loop(0, nch2, score_chunk, 0)

    def threshold(r):
        rows = slice(r * Q_BLOCK, (r + 1) * Q_BLOCK)
        nch_r = i * qg + r + 1

        def count(pred_fn):
            def body(c, cnt):
                return cnt + jnp.where(pred_fn(key_scr[c, rows, :]), 1.0, 0.0)
            cnt = lax.fori_loop(0, nch_r, body, jnp.zeros((Q_BLOCK, LANES), F32))
            return jnp.sum(cnt, axis=1, keepdims=True)

        def bs_pass(b, ans):
            cand = ans + lax.shift_left(jnp.int32(1), jnp.int32(31) - b)
            candb = jnp.broadcast_to(cand, (Q_BLOCK, LANES))
            tot = count(lambda key: key >= candb)
            return jnp.where(tot >= float(topk), cand, ans)

        ans = lax.fori_loop(0, 32, bs_pass, jnp.full((Q_BLOCK, 1), INT_MIN, I32))
        ansb = jnp.broadcast_to(ans, (Q_BLOCK, LANES))
        n_gt = count(lambda key: key > ansb)
        need = jnp.where(ans == jnp.int32(INT_MIN), 0.0, float(topk) - n_gt)
        return ansb, jnp.broadcast_to(need, (Q_BLOCK, LANES))

    parts = [threshold(r) for r in range(qg)]
    ansb = jnp.concatenate([p[0] for p in parts], axis=0)
    needb = jnp.concatenate([p[1] for p in parts], axis=0)

    r2 = lax.broadcasted_iota(I32, (LANES, 2 * LANES), 0)
    c2 = lax.broadcasted_iota(I32, (LANES, 2 * LANES), 1)
    pre_tot = jnp.where((r2 < c2) | (c2 >= LANES), 1.0, 0.0).astype(BF16)

    ties_scr[...] = jnp.zeros_like(ties_scr)

    def bias_chunk(c, carry):
        key = key_scr[c]
        eq = key == ansb
        pt = jnp.dot(jnp.where(eq, 1.0, 0.0).astype(BF16), pre_tot, preferred_element_type=F32)
        seen = ties_scr[...]
        tie_ok = (seen + pt[:, :LANES]) < needb
        bias_scr[c] = jnp.where(key > ansb, 0.0,
                                jnp.where(eq, jnp.where(tie_ok, 0.0, NEG_BIG), NEG_BIG))
        ties_scr[...] = seen + pt[:, LANES:]
        return carry

    lax.fori_loop(0, nch, bias_chunk, 0)
    bias_scr[nch] = jnp.full((qb, LANES), NEG_BIG, F32)

    def logits(c, g, bias4):
        base = pl.multiple_of(c * KCH, KCH)
        s = lax.dot_general(qs_scr[g], k_ref[pl.ds(base, KCH), g * LANES:(g + 1) * LANES],
                            NT_DIMS, preferred_element_type=F32)
        return s * SCALE + bias4

    def bias_rows(c):
        b2 = jnp.concatenate([bias_scr[2 * c], bias_scr[2 * c + 1]], axis=1)
        return jnp.tile(b2, (GROUP, 1))

    mx_scr[...] = jnp.full(mx_scr.shape, NEG_BIG, F32)

    def max_chunk(c, carry):
        bias4 = bias_rows(c)
        for g in range(N_KV_HEADS):
            s = logits(c, g, bias4)
            mx_scr[g] = jnp.maximum(mx_scr[g], jnp.maximum(s[:, :LANES], s[:, LANES:]))
        return carry

    lax.fori_loop(0, nch2, max_chunk, 0)
    for g in range(N_KV_HEADS):
        mx_scr[g] = jnp.broadcast_to(jnp.max(mx_scr[g], axis=1, keepdims=True), mx_scr.shape[1:])

    acc_scr[...] = jnp.zeros_like(acc_scr)

    def acc_chunk(c, carry):
        bias4 = bias_rows(c)
        base = pl.multiple_of(c * KCH, KCH)
        for g in range(N_KV_HEADS):
            m = mx_scr[g]
            p = jnp.exp(logits(c, g, bias4) - jnp.concatenate([m, m], axis=1))
            acc_scr[g] += jnp.dot(p.astype(BF16), ve_ref[pl.ds(base, KCH), g * KCH:(g + 1) * KCH],
                                  preferred_element_type=F32)
        return carry

    lax.fori_loop(0, nch2, acc_chunk, 0)

    for hh in range(N_HEADS):
        g, j = divmod(hh, GROUP)
        a = acc_scr[g, j * qb:(j + 1) * qb, :]
        o_ref[:, hh * LANES:(hh + 1) * LANES] = (a[:, :LANES] / a[:, LANES:]).astype(BF16)


def _pad_keys(a, batch, tk):
    t_pad = a.shape[0] // batch
    a = a.reshape(batch, t_pad, a.shape[1])
    return jnp.pad(a, ((0, 0), (0, tk - t_pad), (0, 0))).reshape(batch * tk, a.shape[2])


def dsa_attention(q, k, v, ai, kb, wi, batch, topk, qb):
    n = q.shape[0]
    t_pad = n // batch
    nq = t_pad // Q_BLOCK
    nb = t_pad // qb
    tk = ((nq + 1) // 2) * KCH
    ones = jnp.ones((n, LANES), v.dtype)
    ve = jnp.concatenate([v[:, :LANES], ones, v[:, LANES:], ones], axis=1)
    k, ve, kb = _pad_keys(k, batch, tk), _pad_keys(ve, batch, tk), _pad_keys(kb, batch, tk)
    blk = lambda b, i: (b * nb + i, 0)
    full = lambda b, i: (b, 0)
    return pl.pallas_call(
        functools.partial(_dsa_body, topk=topk, qb=qb),
        grid=(batch, nb),
        in_specs=[pl.BlockSpec((qb, A_Q), blk), pl.BlockSpec((qb, A_AI), blk),
                  pl.BlockSpec((qb, LANES), blk),
                  pl.BlockSpec((tk, A_KV), full), pl.BlockSpec((tk, 2 * KCH), full),
                  pl.BlockSpec((tk, A_KB), full)],
        out_specs=pl.BlockSpec((qb, A_Q), blk),
        out_shape=jax.ShapeDtypeStruct((n, A_Q), BF16),
        scratch_shapes=[pltpu.VMEM((nq + 2, qb, LANES), I32),
                        pltpu.VMEM((nq + 2, qb, LANES), F32),
                        pltpu.VMEM((IDX_HEADS, qb, KCH), F32),
                        pltpu.VMEM((N_KV_HEADS, GROUP * qb, LANES), BF16),
                        pltpu.VMEM((N_KV_HEADS, GROUP * qb, LANES), F32),
                        pltpu.VMEM((N_KV_HEADS, GROUP * qb, KCH), F32),
                        pltpu.VMEM((qb, LANES), F32)],
        compiler_params=_params(("parallel", "arbitrary")),
        name="dsa_attention",
    )(q, ai, wi, k, ve, kb)


def _sb_body(q_ref, k_ref, v_ref, o_ref, qs_scr, run_scr, acc_scr):
    i = pl.program_id(1)
    for hh in range(N_HEADS):
        g, j = divmod(hh, GROUP)
        qs_scr[g, j * Q_BLOCK:(j + 1) * Q_BLOCK, :] = q_ref[:, hh * LANES:(hh + 1) * LANES]
    run_scr[...] = jnp.zeros_like(run_scr)
    acc_scr[...] = jnp.zeros_like(acc_scr)

    r2 = lax.broadcasted_iota(I32, (2 * LANES, 2 * LANES), 0) % LANES
    c2 = lax.broadcasted_iota(I32, (2 * LANES, 2 * LANES), 1)
    suf_tot = jnp.where((r2 > c2) | (c2 >= LANES), 1.0, 0.0).astype(BF16)

    def step(c, causal):
        base = pl.multiple_of(c * LANES, LANES)
        for g in range(N_KV_HEADS):
            z = lax.dot_general(qs_scr[g], k_ref[pl.ds(base, LANES), g * LANES:(g + 1) * LANES],
                                NT_DIMS, preferred_element_type=F32) * SCALE
            sp = jnp.maximum(z, 0.0) + jnp.log1p(jnp.exp(-jnp.abs(z)))
            lk = -sp if causal is None else jnp.where(causal, -sp, 0.0)
            hi = lk.astype(BF16)
            lo = (lk - hi.astype(F32)).astype(BF16)
            st = jnp.dot(jnp.concatenate([hi, lo], axis=1), suf_tot, preferred_element_type=F32)
            a = jnp.exp(z - sp + st[:, :LANES] + run_scr[g])
            if causal is not None:
                a = jnp.where(causal, a, 0.0)
            acc_scr[g] += jnp.dot(a.astype(BF16), v_ref[pl.ds(base, LANES), g * LANES:(g + 1) * LANES],
                                  preferred_element_type=F32)
            run_scr[g] += st[:, LANES:]

    row = lax.broadcasted_iota(I32, (GROWS, LANES), 0) % Q_BLOCK
    col = lax.broadcasted_iota(I32, (GROWS, LANES), 1)
    step(i, col < row)

    def cond(state):
        t, live = state
        return jnp.logical_and(t <= i, live)

    def body(state):
        t, _ = state
        step(i - t, None)
        worst = jnp.max(jnp.maximum(run_scr[0], run_scr[1]))
        return t + 1, worst > -EXP_UNDERFLOW

    lax.while_loop(cond, body, (jnp.int32(1), jnp.bool_(True)))

    for hh in range(N_HEADS):
        g, j = divmod(hh, GROUP)
        o_ref[:, hh * LANES:(hh + 1) * LANES] = acc_scr[g, j * Q_BLOCK:(j + 1) * Q_BLOCK, :].astype(BF16)


def sb_attention(q, kv, batch):
    n = q.shape[0]
    t_pad = n // batch
    nq = t_pad // Q_BLOCK
    blk = lambda b, i: (b * nq + i, 0)
    return pl.pallas_call(
        _sb_body,
        grid=(batch, nq),
        in_specs=[pl.BlockSpec((Q_BLOCK, A_Q), blk),
                  pl.BlockSpec((t_pad, A_KV), lambda b, i: (b, 0)),
                  pl.BlockSpec((t_pad, A_KV), lambda b, i: (b, 1))],
        out_specs=pl.BlockSpec((Q_BLOCK, A_Q), blk),
        out_shape=jax.ShapeDtypeStruct((n, A_Q), BF16),
        scratch_shapes=[pltpu.VMEM((N_KV_HEADS, GROWS, LANES), BF16),
                        pltpu.VMEM((N_KV_HEADS, GROWS, LANES), F32),
                        pltpu.VMEM((N_KV_HEADS, GROWS, LANES), F32)],
        compiler_params=_params(("parallel", "arbitrary")),
        name="sb_attention",
    )(q, kv, kv)


def _extract_top(s, kidx, pidx):
    cur = s
    rank = jnp.full(s.shape, float(PEER_TOPK), F32)
    tops = jnp.zeros((PEER_TOPK, s.shape[1]), F32)
    for r in range(PEER_TOPK):
        m = jnp.max(cur, axis=0, keepdims=True)
        first = jnp.min(jnp.where(cur == m, kidx, float(N_KEYS)), axis=0, keepdims=True)
        hit = kidx == first
        rank = jnp.where(hit, float(r), rank)
        cur = jnp.where(hit, -jnp.inf, cur)
        tops = jnp.where(pidx == float(r), m, tops)
    return rank, tops


SUBLANES = 8


def _bitonic_merge(v, start):
    j = start
    while j >= 1:
        for i in range(len(v)):
            if i ^ j > i:
                v[i], v[i ^ j] = jnp.maximum(v[i], v[i ^ j]), jnp.minimum(v[i], v[i ^ j])
        j //= 2


def _top_values(s):
    n = N_KEYS // SUBLANES
    assert n == PEER_TOPK
    v = [s[k * SUBLANES:(k + 1) * SUBLANES, :] for k in range(n)]
    k = 2
    while k <= n:
        j = k // 2
        while j >= 1:
            for i in range(n):
                if i ^ j > i:
                    hi, lo = jnp.maximum(v[i], v[i ^ j]), jnp.minimum(v[i], v[i ^ j])
                    v[i], v[i ^ j] = (hi, lo) if (i & k) == 0 else (lo, hi)
            j //= 2
        k *= 2
    shift = SUBLANES // 2
    while shift >= 1:
        v = [jnp.maximum(v[k], pltpu.roll(v[n - 1 - k], shift, 0)) for k in range(n)]
        _bitonic_merge(v, n // 2)
        shift //= 2
    return v


def _rank_by_value(s, tops, pidx):
    n = len(tops)
    groups, at_least = [], jnp.zeros((SUBLANES, s.shape[1]), F32)
    for g in range(N_KEYS // SUBLANES):
        sg = s[g * SUBLANES:(g + 1) * SUBLANES, :]
        r = jnp.zeros_like(sg)
        for k in range(n):
            r = r + jnp.where(tops[k] > sg, 1.0, 0.0)
        groups.append(r)
        at_least = at_least + jnp.where(sg >= tops[n - 1], 1.0, 0.0)
    tied = jnp.where(jnp.sum(at_least, axis=0, keepdims=True) == float(n), 0.0, 1.0)
    vals = jnp.zeros((n, s.shape[1]), F32)
    for k in range(n):
        vals = jnp.where(pidx == float(k), jnp.tile(tops[k], (n // SUBLANES, 1)), vals)
        if k + 1 < n:
            tied = jnp.maximum(tied, jnp.max(jnp.where(tops[k] > tops[k + 1], 0.0, 1.0), axis=0, keepdims=True))
    return jnp.concatenate(groups, axis=0), vals, tied


def _bf16_bits(x):
    return pltpu.bitcast(x.astype(BF16).astype(F32), U32)


def _route_body(q_ref, sk_ref, el_ref, e2_ref, r2_ref, score_scr, rank_scr, vals_scr, *, group):
    kidx = lax.broadcasted_iota(I32, (N_KEYS, LANES), 0).astype(F32)
    pidx = lax.broadcasted_iota(I32, (PEER_TOPK, LANES), 0).astype(F32)

    def select(hh, slot):
        tied = jnp.zeros((1, LANES), F32)
        for c in range(2):
            s = lax.dot_general(sk_ref[c], q_ref[2 * hh + c], NT_DIMS, preferred_element_type=F32)
            score_scr[slot, c] = s
            rank_scr[slot, c], vals_scr[slot, c], t = _rank_by_value(s, _top_values(s), pidx)
            tied = jnp.maximum(tied, t)
        return tied

    def finish(hh, slot):
        s1, s2 = score_scr[slot, 0], score_scr[slot, 1]
        rank1, a = rank_scr[slot, 0], vals_scr[slot, 0]
        rank2, b = rank_scr[slot, 1], vals_scr[slot, 1]

        def take(_, st):
            qmax, head_val = st
            m = jnp.max(head_val, axis=0, keepdims=True)
            first = jnp.min(jnp.where(head_val == m, pidx, float(PEER_TOPK)), axis=0, keepdims=True)
            hit = pidx == first
            qmax = jnp.where(hit, qmax + 1.0, qmax)
            qsel = jnp.sum(jnp.where(hit, qmax, 0.0), axis=0, keepdims=True)
            bnext = jnp.sum(jnp.where(pidx == qsel, b, 0.0), axis=0, keepdims=True)
            bnext = jnp.where(qsel >= float(PEER_TOPK), -jnp.inf, bnext)
            return qmax, jnp.where(hit, a + bnext, head_val)

        qmax, _ = lax.fori_loop(0, PEER_TOPK, take,
                                (jnp.zeros((PEER_TOPK, LANES), F32), a + b[0:1, :]),
                                unroll=True)

        ea = jnp.exp(a - a[0:1, :])
        eb = jnp.exp(b - b[0:1, :])
        pref = jnp.zeros((PEER_TOPK, LANES), F32)
        run = jnp.zeros((1, LANES), F32)
        for q in range(PEER_TOPK):
            run = run + eb[q:q + 1, :]
            pref = jnp.where(qmax == float(q + 1), run, pref)
        z = jnp.sum(ea * pref, axis=0, keepdims=True)

        lim = jnp.zeros((N_KEYS, LANES), F32)
        for p in range(PEER_TOPK):
            lim = jnp.where(rank1 == float(p), qmax[p:p + 1, :], lim)
        e1 = jnp.where(rank1 < float(PEER_TOPK), jnp.exp(s1 - a[0:1, :]), 0.0) / z
        el_ref[hh] = _bf16_bits(e1) | (_bf16_bits(lim) >> 16)
        e2_ref[hh] = jnp.where(rank2 < float(PEER_TOPK), jnp.exp(s2 - b[0:1, :]), 0.0)
        r2_ref[hh] = rank2

    def head_group(hg, carry):
        tied = jnp.zeros((1, LANES), F32)
        for slot in range(group):
            tied = jnp.maximum(tied, select(hg * group + slot, slot))

        @pl.when(jnp.max(tied) > 0.0)
        def _():
            for slot in range(group):
                for c in range(2):
                    rank_scr[slot, c], vals_scr[slot, c] = _extract_top(score_scr[slot, c], kidx, pidx)

        for slot in range(group):
            finish(hg * group + slot, slot)
        return carry

    lax.fori_loop(0, PEER_HEADS // group, head_group, 0)


def peer_route(q, subkeys):
    nhc, n, _ = q.shape
    tab = pl.BlockSpec((PEER_HEADS, N_KEYS, LANES), lambda t: (0, 0, t))
    words = jax.ShapeDtypeStruct((PEER_HEADS, N_KEYS, n), U32)
    halves = jax.ShapeDtypeStruct((PEER_HEADS, N_KEYS, n), F32)
    group = 4
    return pl.pallas_call(
        functools.partial(_route_body, group=group),
        grid=(n // LANES,),
        in_specs=[pl.BlockSpec((nhc, LANES, LANES), lambda t: (0, t, 0)),
                  pl.BlockSpec((2, N_KEYS, PEER_HALF), lambda t: (0, 0, 0))],
        out_specs=[tab, tab, tab],
        out_shape=[words, halves, halves],
        scratch_shapes=[pltpu.VMEM((group, 2, N_KEYS, LANES), F32),
                        pltpu.VMEM((group, 2, N_KEYS, LANES), F32),
                        pltpu.VMEM((group, 2, PEER_TOPK, LANES), F32)],
        compiler_params=_params(("parallel",)),
        name="peer_route",
    )(q, subkeys)


def _gelu(x):
    return 0.5 * x * (1.0 + lax.erf(x * (2.0 ** -0.5)))


def _expert_body(h_ref, g_ref, u_ref, v_ref, el_ref, e2_ref, r2_ref, out_ref,
                 xt_scr, acc_scr, ht_scr, pt_scr, *, tm, eb, ne):
    e = pl.program_id(1)
    nlc = tm // LANES
    ndc = acc_scr.shape[0]

    nii = eb // N_KEYS

    def mm1(slab):
        res = jnp.dot(u_ref[...], xt_scr[...], preferred_element_type=F32)
        for tc in range(nlc):
            ht_scr[slab % 2, tc] = res[:, tc * LANES:(tc + 1) * LANES]

    def mm2(slab):
        lhs = jnp.concatenate([pt_scr[slab % 2, tc] for tc in range(nlc)], axis=1)
        res = lax.dot_general(lhs, v_ref[...], (((0,), (0,)), ((), ())), preferred_element_type=F32)
        for dc in range(ndc):
            acc_scr[dc] += res[:, dc * LANES:(dc + 1) * LANES]

    def gate(slab):
        par = slab % 2
        for ii in range(nii):
            el_i = el_ref[slab * nii + ii]
            e1_i = pltpu.bitcast(el_i & jnp.uint32(0xFFFF0000), F32)
            lim_i = pltpu.bitcast(el_i << 16, F32)
            for tc in range(nlc):
                sl = slice(tc * LANES, (tc + 1) * LANES)
                g = jnp.zeros((N_KEYS, LANES), F32)
                for hh in range(PEER_HEADS):
                    picked = jnp.where(r2_ref[hh, :, sl] < lim_i[hh:hh + 1, sl], e2_ref[hh, :, sl], 0.0)
                    g = g + picked * e1_i[hh:hh + 1, sl]
                act = _gelu(ht_scr[par, tc, ii * N_KEYS:(ii + 1) * N_KEYS, :])
                pt_scr[par, tc, ii * N_KEYS:(ii + 1) * N_KEYS, :] = (act * g).astype(BF16)

    @pl.when(e == 0)
    def _():
        y = _rms(h_ref[...], g_ref[...])
        xt_scr[...] = y.T.astype(BF16)
        acc_scr[...] = jnp.zeros_like(acc_scr)
        mm1(e)

    @pl.when(e == 1)
    def _():
        mm1(e)
        gate(e - 1)

    @pl.when(jnp.logical_and(e >= 2, e < ne))
    def _():
        mm1(e)
        mm2(e - 2)
        gate(e - 1)

    @pl.when(e == ne)
    def _():
        mm2(e - 2)
        gate(e - 1)

    @pl.when(e == ne + 1)
    def _():
        mm2(e - 2)
        out_ref[...] = h_ref[...] + jnp.concatenate([acc_scr[dc] for dc in range(ndc)], axis=1)


def peer_experts(h, gain, u, v, tabs, tm, eb):
    n, d = h.shape
    ne = u.shape[0] // eb
    nlc = tm // LANES
    tab = pl.BlockSpec((PEER_HEADS, N_KEYS, tm), lambda t, e: (0, 0, t))
    rtab = pl.BlockSpec((N_KEYS, PEER_HEADS, tm), lambda t, e: (0, 0, t))
    el, e2, r2 = tabs
    el = jnp.transpose(el, (1, 0, 2))
    return pl.pallas_call(
        functools.partial(_expert_body, tm=tm, eb=eb, ne=ne),
        grid=(n // tm, ne + 2),
        in_specs=[pl.BlockSpec((tm, d), lambda t, e: (t, 0)),
                  pl.BlockSpec((1, d), lambda t, e: (0, 0)),
                  pl.BlockSpec((eb, d), lambda t, e: (jnp.minimum(e, ne - 1), 0)),
                  pl.BlockSpec((eb, d), lambda t, e: (jnp.clip(e - 2, 0, ne - 1), 0)),
                  rtab, tab, tab],
        out_specs=pl.BlockSpec((tm, d), lambda t, e: (t, 0)),
        out_shape=jax.ShapeDtypeStruct((n, d), F32),
        scratch_shapes=[pltpu.VMEM((d, tm), BF16), pltpu.VMEM((d // LANES, tm, LANES), F32),
                        pltpu.VMEM((2, nlc, eb, LANES), F32), pltpu.VMEM((2, nlc, eb, LANES), BF16)],
        compiler_params=_params(("parallel", "arbitrary")),
        name="peer_experts",
    )(h, gain.reshape(1, d), u, v, el, e2, r2)


def peer_ffn(h, gain, w_q, subkeys, u, v, tm_q, tm_e, eb):
    q = peer_query(h, gain, w_q, tm_q)
    tabs = peer_route(q, subkeys)
    return peer_experts(h, gain, u, v, tabs, tm_e, eb)


def _rope_tables(t_pad, dim, period):
    r = dim // 4
    half = r // 2
    pos = jnp.arange(t_pad, dtype=F32)
    inv = ROPE_THETA ** (-jnp.arange(half, dtype=F32) * 2.0 / r)
    ang = pos[:, None] * inv[None, :]
    cos, sin = jnp.cos(ang), jnp.sin(ang)
    pad = period - 2 * half
    c = jnp.concatenate([cos, cos, jnp.ones((t_pad, pad), F32)], axis=1)
    sa = jnp.concatenate([-sin, jnp.zeros((t_pad, half + pad), F32)], axis=1)
    sb = jnp.concatenate([jnp.zeros((t_pad, half), F32), sin, jnp.zeros((t_pad, pad), F32)], axis=1)
    reps = LANES // period
    return tuple(jnp.tile(a, (1, reps)) for a in (c, sa, sb))


def _a_weights(w_in):
    d = w_in.shape[0]
    lo = A_Q + 2 * A_KV
    zeros = jnp.zeros((d, IDX_DIM), w_in.dtype)
    cols = [w_in[:, :lo]]
    for hh in range(IDX_HEADS):
        qi = w_in[:, lo + hh * IDX_DIM:lo + (hh + 1) * IDX_DIM]
        cols += [qi, qi, qi, zeros]
    lo += IDX_HEADS * IDX_DIM
    ki = w_in[:, lo:lo + IDX_DIM]
    cols += [ki, ki, ki, zeros]
    lo += IDX_DIM
    cols += [w_in[:, lo:lo + IDX_HEADS], jnp.zeros((d, LANES - IDX_HEADS), w_in.dtype)]
    return jnp.concatenate(cols, axis=1).astype(BF16)


def kernel(x, meta_tokens, a_norm, a_w_in, a_q_norm, a_k_norm, a_w_o, kv_norm, kv_w,
           b_norm, b_w_q, b_w_o, ffn_norm, peer_w_q, peer_subkeys, peer_u, peer_v):
    b, s, d = x.shape
    t = s + N_META
    t_pad = -(-t // Q_BLOCK) * Q_BLOCK
    n = b * t_pad
    depth = ffn_norm.shape[0]
    n_a = a_norm.shape[0]
    topk = min(INDEX_TOPK, s // 4)

    h = jnp.concatenate([jnp.broadcast_to(meta_tokens[None].astype(x.dtype), (b, N_META, d)), x], axis=1)
    h = jnp.pad(h, ((0, 0), (0, t_pad - t), (0, 0))).reshape(n, d)

    tabs = _rope_tables(t_pad, HEAD_DIM, LANES) + _rope_tables(t_pad, IDX_DIM, IDX_DIM)
    tm_a = t_pad // 11 if t_pad % 11 == 0 and (t_pad // 11) % 8 == 0 else Q_BLOCK
    tm = 512 if n % 512 == 0 else Q_BLOCK
    tm_e = 768 if n % 768 == 0 else tm
    qb_a = 3 * Q_BLOCK if t_pad % (3 * Q_BLOCK) == 0 else Q_BLOCK

    kv = None
    for layer in range(depth):
        if layer < n_a:
            q, k, v, ai, kb, wi = a_proj(h, a_norm[layer], _a_weights(a_w_in[layer]),
                                         a_q_norm[layer], a_k_norm[layer], tabs, b, tm_a)
            o = dsa_attention(q, k, v, ai, kb, wi, b, topk, qb_a)
            h = out_proj(h, o, a_w_o[layer].astype(BF16), tm)
        else:
            j = layer - n_a
            q = rms_proj(h, b_norm[j], b_w_q[j].astype(BF16), tm, BF16)
            o = sb_attention(q, kv, b)
            h = out_proj(h, o, b_w_o[j].astype(BF16), tm)
        h = peer_ffn(h, ffn_norm[layer], peer_w_q[layer].astype(BF16),
                     peer_subkeys[layer].astype(BF16), peer_u[layer].astype(BF16),
                     peer_v[layer].astype(BF16), tm, tm_e, 4 * N_KEYS)
        if layer == n_a - 1:
            kv = rms_proj(h, kv_norm, kv_w.astype(BF16), tm, BF16)
    return h.reshape(b, t_pad, d)[:, N_META:N_META + s]
```
